```python
import jax, jax.numpy as jnp
from jax import lax
import numpy as np

D_MODEL = 2048
BATCH = 2
SEQ = 4096
DEPTH = 2
DEC_BATCH = 128
DEC_SEQ = 8
PAST_LEN = 8192
PAGE_SIZE = 128

HEAD_DIM = 128
N_MEM_HEADS = 4
N_MEM_TOKENS = 256
N_HEADS_A = D_MODEL // HEAD_DIM - N_MEM_HEADS
N_KV_A = 2
HPG_A = N_HEADS_A // N_KV_A
CMP_STRIDE = 16
CMP_LEN = 2 * CMP_STRIDE
SEL_BLK = 64
N_SEL = 16
WIN_A = 512
N_HEADS_B = D_MODEL // HEAD_DIM - N_MEM_HEADS
N_KV_B = 2
HPG_B = N_HEADS_B // N_KV_B
WIN_B = 128
BAND_BLK = 128
MIX_WIDTH = (N_HEADS_A + N_MEM_HEADS) * HEAD_DIM
Q_A = N_HEADS_A * HEAD_DIM
KV_A = 2 * N_KV_A * HEAD_DIM
Q_B = N_HEADS_B * HEAD_DIM
KV_B = 2 * N_KV_B * HEAD_DIM
Q_MEM = N_MEM_HEADS * HEAD_DIM
IN_A = Q_A + 3 * KV_A + 3 * N_HEADS_A + Q_MEM
IN_B = Q_B + KV_B + Q_MEM
PEER_HEADS = 8
PEER_KEYS = 128
PEER_EXPERTS = PEER_KEYS * PEER_KEYS
PEER_TOPK = 16
PEER_QDIM = 256
PEER_BLK = 128
LN_EPS = 1e-5
NEG_INF = -1e30
SEL_FORCE = 1e4
SCALE = HEAD_DIM ** -0.5

kernel_name = "nsa_sinkswa_peer_hybrid_step"


def _layer_norm(x, g, b):
    xf = x.astype(jnp.float32)
    mu = xf.mean(-1, keepdims=True)
    var = ((xf - mu) ** 2).mean(-1, keepdims=True)
    return ((xf - mu) * lax.rsqrt(var + LN_EPS) * g.astype(jnp.float32) + b.astype(jnp.float32)).astype(x.dtype)


def _alibi_slopes(n_heads, n_groups):
    i = jnp.arange(1, n_heads + 1, dtype=jnp.float32)
    return jnp.exp2(-8.0 * i / n_heads).reshape(n_groups, n_heads // n_groups)


def _masked_softmax(s, mask):
    s = jnp.where(mask, s, NEG_INF)
    m = jnp.max(s, -1, keepdims=True)
    e = jnp.where(mask, jnp.exp(s - m), 0.0)
    return e / jnp.maximum(e.sum(-1, keepdims=True), 1e-30)


def _rows(kv, pos):
    bi = jnp.arange(kv.shape[0]).reshape(-1, 1, 1, 1, 1)
    gi = jnp.arange(kv.shape[3]).reshape(1, 1, -1, 1, 1)
    return kv[bi, pos[..., None], jnp.arange(2), gi]


def _chunk_proj(kv, w_cmp):
    B, L = kv.shape[:2]
    nch = L // CMP_STRIDE
    if nch * CMP_STRIDE != L:
        kv = kv[:, :nch * CMP_STRIDE]
    ch = kv.reshape((B, nch, CMP_STRIDE) + kv.shape[2:])
    a = jnp.einsum('bnrcgd,crde->bncge', ch, w_cmp[:, :CMP_STRIDE])
    b = jnp.einsum('bnrcgd,crde->bncge', ch, w_cmp[:, CMP_STRIDE:])
    return a, b


def _compress(a, b, w_cmp, pe_cmp):
    bias = jnp.einsum('crd,crde->ce', pe_cmp, w_cmp)
    return a[:, :-1] + b[:, 1:] + bias[None, None, :, None, :]


def _cmp_attend(q, kvc, q_pos, slopes):
    k, v = kvc[:, :, 0], kvc[:, :, 1]
    end = jnp.arange(k.shape[1]) * CMP_STRIDE + (CMP_LEN - 1)
    dist = (q_pos[:, None] - end[None, :]).astype(jnp.float32)
    s = jnp.einsum('btghd,bngd->btghn', q, k).astype(jnp.float32) * SCALE
    s = s - slopes[None, None, :, :, None] * dist[None, :, None, None, :]
    p = _masked_softmax(s, (dist >= 0)[None, :, None, None, :])
    o = jnp.einsum('btghn,bngd->btghd', p.astype(v.dtype), v)
    return o, p


def _select_blocks(p_cmp, q_pos, seq_len):
    imp_c = p_cmp.sum(axis=3)
    n_cmp = imp_c.shape[-1]
    n_blk = -(-seq_len // SEL_BLK)
    c0 = jnp.arange(n_cmp)[:, None] * CMP_STRIDE
    s0 = jnp.arange(n_blk)[None, :] * SEL_BLK
    ov = ((c0 <= s0 + SEL_BLK - 1) & (c0 + CMP_LEN - 1 >= s0)).astype(jnp.float32)
    imp = jnp.einsum('btgn,nj->btgj', imp_c, ov)
    blk = jnp.arange(n_blk)[None, :]
    cur = (q_pos // SEL_BLK)[:, None]
    forced = (blk == 0) | (blk == cur) | (blk == cur - 1)
    valid = blk <= cur
    score = jnp.where(forced[None, :, None, :], SEL_FORCE,
                      jnp.where(valid[None, :, None, :], imp, -SEL_FORCE))
    _, idx = lax.top_k(score, min(N_SEL, n_blk))
    return idx


def _sel_attend(q, idx, q_pos, slopes, gather):
    pos = (idx[..., None] * SEL_BLK + jnp.arange(SEL_BLK)).reshape(idx.shape[:-1] + (-1,))
    kv = gather(pos)
    k, v = kv[..., 0, :], kv[..., 1, :]
    dist = (q_pos[None, :, None, None] - pos).astype(jnp.float32)
    s = jnp.einsum('btghd,btgnd->btghn', q, k).astype(jnp.float32) * SCALE
    s = s - slopes[None, None, :, :, None] * dist[:, :, :, None, :]
    p = _masked_softmax(s, (dist >= 0)[:, :, :, None, :])
    return jnp.einsum('btghn,btgnd->btghd', p.astype(v.dtype), v)


def _sel_attend_blocked(q, idx, q_pos, slopes, gather, blk):
    B, T = q.shape[:2]
    nb = T // blk
    qb = jnp.moveaxis(q.reshape((B, nb, blk) + q.shape[2:]), 1, 0)
    ib = jnp.moveaxis(idx.reshape((B, nb, blk) + idx.shape[2:]), 1, 0)
    pb = q_pos.reshape(nb, blk)
    ob = lax.map(lambda a: _sel_attend(a[0], a[1], a[2], slopes, gather), (qb, ib, pb))
    return jnp.moveaxis(ob, 0, 1).reshape(q.shape)


def _band_keys(kv, window):
    B, T = kv.shape[:2]
    nw = window // BAND_BLK
    nqb = T // BAND_BLK
    pad = jnp.zeros((B, nw * BAND_BLK) + kv.shape[2:], kv.dtype)
    kb = jnp.concatenate([pad, kv], 1).reshape((B, nqb + nw, BAND_BLK) + kv.shape[2:])
    blocks = jnp.concatenate([kb[:, j:j + nqb] for j in range(nw + 1)], axis=2)
    k_pos = (jnp.arange(nqb)[:, None] - nw) * BAND_BLK + jnp.arange((nw + 1) * BAND_BLK)[None, :]
    return blocks[:, :, :, 0], blocks[:, :, :, 1], k_pos


def _band_attend(q, k, v, q_pos, k_pos, window, slopes, sinks):
    s = jnp.einsum('bnqghd,bnkgd->bnghqk', q, k).astype(jnp.float32) * SCALE
    dist = q_pos[:, :, None] - k_pos[:, None, :]
    mask = ((dist >= 0) & (dist < window) & (k_pos[:, None, :] >= 0))[None, :, None, None]
    s = s - slopes[None, None, :, :, None, None] * dist.astype(jnp.float32)[None, :, None, None]
    if sinks is None:
        p = _masked_softmax(s, mask)
    else:
        s = jnp.where(mask, s, NEG_INF)
        snk = sinks[None, None, :, :, None, None]
        m = jnp.maximum(s.max(-1, keepdims=True), snk)
        e = jnp.where(mask, jnp.exp(s - m), 0.0)
        p = e / (e.sum(-1, keepdims=True) + jnp.exp(snk - m))
    return jnp.einsum('bnghqk,bnkgd->bnqghd', p.astype(v.dtype), v)


def _split_a(h, b_gate):
    B, T = h.shape[:2]
    q = h[..., :Q_A].reshape(B, T, N_KV_A, HPG_A, HEAD_DIM)
    kvs = [h[..., Q_A + i * KV_A:Q_A + (i + 1) * KV_A].reshape(B, T, 2, N_KV_A, HEAD_DIM) for i in range(3)]
    g0 = Q_A + 3 * KV_A
    gates = jax.nn.sigmoid((h[..., g0:g0 + 3 * N_HEADS_A] + b_gate).astype(jnp.float32))
    gates = gates.astype(h.dtype).reshape(B, T, N_KV_A, HPG_A, 3)
    qm = h[..., g0 + 3 * N_HEADS_A:].reshape(B, T, N_MEM_HEADS, HEAD_DIM)
    return q, kvs[0], kvs[1], kvs[2], gates, qm


def _nsa_combine(gates, o_c, o_s, o_w):
    B, T = o_c.shape[:2]
    o = gates[..., 0:1] * o_c + gates[..., 1:2] * o_s + gates[..., 2:3] * o_w
    return o.reshape(B, T, -1)


def _nsa_prompt(h, b_gate, w_cmp, pe_cmp):
    q, kv_c, kv_s, kv_w, gates, qm = _split_a(h, b_gate)
    B, T = q.shape[:2]
    q_pos = jnp.arange(T)
    slopes = _alibi_slopes(N_HEADS_A, N_KV_A)
    a, b = _chunk_proj(kv_c, w_cmp)
    o_c, p_c = _cmp_attend(q, _compress(a, b, w_cmp, pe_cmp), q_pos, slopes)
    idx = _select_blocks(p_c, q_pos, T)
    gather = lambda pos: _rows(kv_s, jnp.clip(pos, 0, T - 1))
    o_s = _sel_attend_blocked(q, idx, q_pos, slopes, gather, BAND_BLK)
    kb, vb, k_pos = _band_keys(kv_w, WIN_A)
    nb = T // BAND_BLK
    o_w = _band_attend(q.reshape(B, nb, BAND_BLK, N_KV_A, HPG_A, HEAD_DIM), kb, vb,
                       q_pos.reshape(nb, BAND_BLK), k_pos, WIN_A, slopes, None).reshape(q.shape)
    return _nsa_combine(gates, o_c, o_s, o_w), qm, (kv_c, kv_s, kv_w[:, -min(WIN_A, T):])


def _nsa_sample(h, b_gate, w_cmp, pe_cmp, cache_cmp, cache_sel, win_buf, page_table, lk):
    q, kv_c, kv_s, kv_w, gates, qm = _split_a(h, b_gate)
    B, T = q.shape[:2]
    q_pos = PAST_LEN + jnp.arange(T)
    slopes = _alibi_slopes(N_HEADS_A, N_KV_A)
    past_c = cache_cmp[lk, page_table].reshape(B, PAST_LEN, 2, N_KV_A, HEAD_DIM)
    a0, b0 = _chunk_proj(past_c, w_cmp)
    a1, b1 = _chunk_proj(kv_c, w_cmp)
    kvc = _compress(jnp.concatenate([a0, a1], 1), jnp.concatenate([b0, b1], 1), w_cmp, pe_cmp)
    o_c, p_c = _cmp_attend(q, kvc, q_pos, slopes)
    idx = _select_blocks(p_c, q_pos, PAST_LEN + T)
    bi = jnp.arange(B).reshape(-1, 1, 1, 1)
    gi = jnp.arange(N_KV_A).reshape(1, 1, -1, 1, 1)

    def gather(pos):
        pc = jnp.minimum(pos, PAST_LEN - 1)
        phys = page_table[bi, pc // PAGE_SIZE]
        rows_p = cache_sel[lk, phys[..., None], (pc % PAGE_SIZE)[..., None], jnp.arange(2), gi]
        rows_n = _rows(kv_s, jnp.clip(pos - PAST_LEN, 0, T - 1))
        return jnp.where((pos < PAST_LEN)[..., None, None], rows_p, rows_n)

    o_s = _sel_attend_blocked(q, idx, q_pos, slopes, gather, 1)
    wb = win_buf.shape[1]
    all_w = jnp.concatenate([win_buf, kv_w], 1)
    k_pos = PAST_LEN - wb + jnp.arange(wb + T)
    o_w = _band_attend(q[:, None], all_w[:, None, :, 0], all_w[:, None, :, 1], q_pos[None], k_pos[None],
                       WIN_A, slopes, None)[:, 0]
    return _nsa_combine(gates, o_c, o_s, o_w), qm, (kv_c, kv_s, all_w[:, -wb:])


def _split_b(h):
    B, T = h.shape[:2]
    q = h[..., :Q_B].reshape(B, T, N_KV_B, HPG_B, HEAD_DIM)
    kv = h[..., Q_B:Q_B + KV_B].reshape(B, T, 2, N_KV_B, HEAD_DIM)
    qm = h[..., Q_B + KV_B:].reshape(B, T, N_MEM_HEADS, HEAD_DIM)
    return q, kv, qm


def _swa_prompt(h, sinks):
    q, kv, qm = _split_b(h)
    B, T = q.shape[:2]
    slopes = _alibi_slopes(N_HEADS_B, N_KV_B)
    snk = sinks.reshape(N_KV_B, HPG_B).astype(jnp.float32)
    kb, vb, k_pos = _band_keys(kv, WIN_B)
    nb = T // BAND_BLK
    o = _band_attend(q.reshape(B, nb, BAND_BLK, N_KV_B, HPG_B, HEAD_DIM), kb, vb,
                     jnp.arange(T).reshape(nb, BAND_BLK), k_pos, WIN_B, slopes, snk)
    return o.reshape(B, T, -1), qm, kv[:, -min(WIN_B, T):]


def _swa_sample(h, sinks, buf):
    q, kv, qm = _split_b(h)
    B, T = q.shape[:2]
    slopes = _alibi_slopes(N_HEADS_B, N_KV_B)
    snk = sinks.reshape(N_KV_B, HPG_B).astype(jnp.float32)
    wb = buf.shape[1]
    all_kv = jnp.concatenate([buf, kv], 1)
    q_pos = PAST_LEN + jnp.arange(T)
    k_pos = PAST_LEN - wb + jnp.arange(wb + T)
    o = _band_attend(q[:, None], all_kv[:, None, :, 0], all_kv[:, None, :, 1], q_pos[None], k_pos[None],
                     WIN_B, slopes, snk)[:, 0]
    return o.reshape(B, T, -1), qm, all_kv[:, -wb:]


def _mem_kv(mem, w):
    B, M = mem.shape[:2]
    return (mem @ w).reshape(B, M, 2, N_MEM_HEADS, HEAD_DIM)


def _mem_attend(qm, mkv):
    k, v = mkv[:, :, 0], mkv[:, :, 1]
    s = jnp.einsum('bthd,bmhd->bhtm', qm, k).astype(jnp.float32) * SCALE
    p = jax.nn.softmax(s, axis=-1)
    o = jnp.einsum('bhtm,bmhd->bthd', p.astype(v.dtype), v)
    return o.reshape(o.shape[:2] + (-1,))


def _peer(x, wq, subkeys, u, v):
    shp = x.shape
    xt = x.reshape(-1, shp[-1])
    n = xt.shape[0]
    nb = -(-n // PEER_BLK)
    xt = jnp.pad(xt, ((0, nb * PEER_BLK - n), (0, 0))).reshape(nb, PEER_BLK, shp[-1])

    def block(xb):
        q = (xb @ wq).reshape(PEER_BLK, PEER_HEADS, 2, PEER_QDIM // 2)
        s = jnp.einsum('thcd,hcnd->thcn', q, subkeys).astype(jnp.float32)
        s1, i1 = lax.top_k(s[:, :, 0], PEER_TOPK)
        s2, i2 = lax.top_k(s[:, :, 1], PEER_TOPK)
        cand = (s1[..., :, None] + s2[..., None, :]).reshape(PEER_BLK, PEER_HEADS, -1)
        cidx = (i1[..., :, None] * PEER_KEYS + i2[..., None, :]).reshape(PEER_BLK, PEER_HEADS, -1)
        sc, sel = lax.top_k(cand, PEER_TOPK)
        e = jnp.take_along_axis(cidx, sel, axis=-1)
        g = jax.nn.softmax(sc, axis=-1)
        act = jax.nn.gelu(jnp.einsum('td,thkd->thk', xb, u[e]).astype(jnp.float32), approximate=False)
        return jnp.einsum('thk,thkd->td', (g * act).astype(v.dtype), v[e])

    out = lax.map(block, xt).reshape(nb * PEER_BLK, shp[-1])[:n]
    return out.reshape(shp)


def setup_inputs(seed: int = 0) -> dict:
    key = jax.random.key(seed)
    ks = jax.random.split(key, 24)
    f32 = jnp.float32
    n_a = (DEPTH + 1) // 2
    n_b = DEPTH // 2
    n_pages = PAST_LEN // PAGE_SIZE
    n_used = DEC_BATCH * n_pages
    n_pool = n_used + n_used // 4
    wb_a = min(WIN_A, PAST_LEN)
    wb_b = min(WIN_B, PAST_LEN)
    beta = (8.0 * DEPTH) ** -0.25

    def nrm(k, shape, std):
        return std * jax.random.normal(k, shape, f32)

    page_table = jax.random.permutation(ks[8], n_pool)[:n_used].reshape(DEC_BATCH, n_pages).astype(jnp.int32)
    return {
        "x_prompt": jax.random.normal(ks[0], (BATCH, SEQ, D_MODEL), f32),
        "x_sample": jax.random.normal(ks[1], (DEC_BATCH, DEC_SEQ, D_MODEL), f32),
        "mem_prompt": jax.random.normal(ks[2], (BATCH, N_MEM_TOKENS, D_MODEL), f32),
        "cache_a_cmp_kv": jax.random.normal(ks[3], (n_a, n_pool, PAGE_SIZE, 2, N_KV_A, HEAD_DIM), f32),
        "cache_a_sel_kv": jax.random.normal(ks[4], (n_a, n_pool, PAGE_SIZE, 2, N_KV_A, HEAD_DIM), f32),
        "cache_a_win_kv": jax.random.normal(ks[5], (n_a, DEC_BATCH, wb_a, 2, N_KV_A, HEAD_DIM), f32),
        "cache_b_kv": jax.random.normal(ks[6], (n_b, DEC_BATCH, wb_b, 2, N_KV_B, HEAD_DIM), f32),
        "cache_mem_kv": jax.random.normal(ks[7], (DEPTH, DEC_BATCH, N_MEM_TOKENS, 2, N_MEM_HEADS, HEAD_DIM), f32),
        "page_table": page_table,
        "w_in_a": nrm(ks[9], (n_a, D_MODEL, IN_A), D_MODEL ** -0.5),
        "b_gate_a": nrm(ks[10], (n_a, 3 * N_HEADS_A), 0.1),
        "w_cmp_a": nrm(ks[11], (n_a, 2, CMP_LEN, HEAD_DIM, HEAD_DIM), (CMP_LEN * HEAD_DIM) ** -0.5),
        "pe_cmp_a": nrm(ks[12], (n_a, 2, CMP_LEN, HEAD_DIM), 0.02),
        "w_in_b": nrm(ks[13], (n_b, D_MODEL, IN_B), D_MODEL ** -0.5),
        "sinks_b": nrm(ks[14], (n_b, N_HEADS_B), 0.5),
        "w_mem_kv": nrm(ks[15], (DEPTH, D_MODEL, 2 * N_MEM_HEADS * HEAD_DIM), D_MODEL ** -0.5),
        "w_out": nrm(ks[16], (DEPTH, MIX_WIDTH, D_MODEL), beta * MIX_WIDTH ** -0.5),
        "b_out": nrm(ks[17], (DEPTH, D_MODEL), 0.01),
        "ln_g": 1.0 + nrm(ks[18], (DEPTH, 2, D_MODEL), 0.01),
        "ln_b": nrm(ks[19], (DEPTH, 2, D_MODEL), 0.01),
        "peer_wq": nrm(ks[20], (DEPTH, D_MODEL, PEER_HEADS * PEER_QDIM), D_MODEL ** -0.5),
        "peer_subkeys": nrm(ks[21], (DEPTH, PEER_HEADS, 2, PEER_KEYS, PEER_QDIM // 2), (PEER_QDIM // 2) ** -0.5),
        "peer_u": nrm(ks[22], (DEPTH, PEER_EXPERTS, D_MODEL), D_MODEL ** -0.5),
        "peer_v": nrm(ks[23], (DEPTH, PEER_EXPERTS, D_MODEL), beta * (PEER_HEADS * PEER_TOPK) ** -0.5),
    }


def reference(x_prompt, x_sample, mem_prompt, cache_a_cmp_kv, cache_a_sel_kv, cache_a_win_kv, cache_b_kv,
              cache_mem_kv, page_table, w_in_a, b_gate_a, w_cmp_a, pe_cmp_a, w_in_b, sinks_b, w_mem_kv,
              w_out, b_out, ln_g, ln_b, peer_wq, peer_subkeys, peer_u, peer_v):
    alpha = (2.0 * DEPTH) ** 0.25
    xp, xs = x_prompt, x_sample
    a_cmp_p, a_cmp_s, a_sel_p, a_sel_s, a_win_p, a_win_s = [], [], [], [], [], []
    b_kv_p, b_kv_s, mem_p = [], [], []
    for li in range(DEPTH):
        lk = li // 2
        mkv_p = _mem_kv(mem_prompt, w_mem_kv[li])
        mkv_s = cache_mem_kv[li]
        if li % 2 == 0:
            mix_p, qm_p, (c_p, s_p, w_p) = _nsa_prompt(xp @ w_in_a[lk], b_gate_a[lk], w_cmp_a[lk], pe_cmp_a[lk])
            mix_s, qm_s, (c_s, s_s, w_s) = _nsa_sample(xs @ w_in_a[lk], b_gate_a[lk], w_cmp_a[lk], pe_cmp_a[lk],
                                                        cache_a_cmp_kv, cache_a_sel_kv, cache_a_win_kv[lk],
                                                        page_table, lk)
            a_cmp_p.append(c_p); a_cmp_s.append(c_s)
            a_sel_p.append(s_p); a_sel_s.append(s_s)
            a_win_p.append(w_p); a_win_s.append(w_s)
        else:
            mix_p, qm_p, kvp = _swa_prompt(xp @ w_in_b[lk], sinks_b[lk])
            mix_s, qm_s, kvs = _swa_sample(xs @ w_in_b[lk], sinks_b[lk], cache_b_kv[lk])
            b_kv_p.append(kvp); b_kv_s.append(kvs)
        mem_p.append(mkv_p)
        yp = jnp.concatenate([mix_p, _mem_attend(qm_p, mkv_p)], -1) @ w_out[li] + b_out[li]
        ys = jnp.concatenate([mix_s, _mem_attend(qm_s, mkv_s)], -1) @ w_out[li] + b_out[li]
        xp = _layer_norm(alpha * xp + yp, ln_g[li, 0], ln_b[li, 0])
        xs = _layer_norm(alpha * xs + ys, ln_g[li, 0], ln_b[li, 0])
        u_l, v_l = peer_u[li], peer_v[li]
        xp = _layer_norm(alpha * xp + _peer(xp, peer_wq[li], peer_subkeys[li], u_l, v_l), ln_g[li, 1], ln_b[li, 1])
        xs = _layer_norm(alpha * xs + _peer(xs, peer_wq[li], peer_subkeys[li], u_l, v_l), ln_g[li, 1], ln_b[li, 1])
    return (xp, xs, jnp.stack(a_cmp_p), jnp.stack(a_cmp_s), jnp.stack(a_sel_p), jnp.stack(a_sel_s),
            jnp.stack(a_win_p), jnp.stack(a_win_s), jnp.stack(b_kv_p), jnp.stack(b_kv_s), jnp.stack(mem_p))
```

```python
import functools
import math

import jax
import jax.numpy as jnp
from jax import lax
from jax.experimental import pallas as pl
from jax.experimental.pallas import tpu as pltpu

F32 = jnp.float32
BF16 = jnp.bfloat16

HEAD_DIM = 128
N_MEM_HEADS = 4
N_KV = 2
HPG = 6
N_HEADS = N_KV * HPG
Q_W = N_HEADS * HEAD_DIM
KV_W = 2 * N_KV * HEAD_DIM
KV_ROWS = 2 * N_KV
QM_W = N_MEM_HEADS * HEAD_DIM
CMP_STRIDE = 16
CMP_LEN = 32
SEL_BLK = 64
N_SEL = 16
WIN_A = 512
WIN_B = 128
PAGE_SIZE = 128
PEER_HEADS = 8
PEER_KEYS = 128
PEER_TOPK = 16
PEER_HALF = 128
LN_EPS = 1e-5
NEG_INF = -1e30
SEL_FORCE = 1e4
SCALE = HEAD_DIM ** -0.5
SQRT_HALF = math.sqrt(0.5)
LANES = 128
PAGES_PER_STEP = 16

A_KC, A_KS, A_KW, A_QM, A_GATE = Q_W, Q_W + KV_W, Q_W + 2 * KV_W, Q_W + 3 * KV_W, Q_W + 3 * KV_W + QM_W
A_WIDTH = A_GATE + N_KV * LANES
B_KV, B_QM = Q_W, Q_W + KV_W
B_WIDTH = B_QM + QM_W


def _params(sem, vmem_mb):
    return pltpu.CompilerParams(dimension_semantics=sem, vmem_limit_bytes=vmem_mb << 20)


def _dot(a, b):
    return jnp.dot(a, b, preferred_element_type=F32)


def _dot_nt(a, b):
    return lax.dot_general(a, b, (((1,), (1,)), ((), ())), preferred_element_type=F32)


def _mm_kernel(x_ref, w_ref, o_ref):
    o_ref[...] = _dot(x_ref[...].astype(BF16), w_ref[...])


def _matmul(x, w, tm, tn):
    m, k = x.shape
    n = w.shape[1]
    return pl.pallas_call(
        _mm_kernel,
        grid=(m // tm, n // tn),
        in_specs=[pl.BlockSpec((tm, k), lambda i, j: (i, 0)), pl.BlockSpec((k, tn), lambda i, j: (0, j))],
        out_specs=pl.BlockSpec((tm, tn), lambda i, j: (i, j)),
        out_shape=jax.ShapeDtypeStruct((m, n), F32),
        compiler_params=_params(("parallel", "parallel"), 40),
        name="proj",
    )(x, w)


def _layer_norm(z, g, b):
    mu = jnp.mean(z, axis=-1, keepdims=True)
    zc = z - mu
    var = jnp.mean(zc * zc, axis=-1, keepdims=True)
    return zc * lax.rsqrt(var + LN_EPS) * g + b


def _outproj_kernel(alpha, mix_ref, mem_ref, w1_ref, w2_ref, bias_ref, x_ref, g_ref, b_ref, o_ref):
    y = _dot(mix_ref[...].astype(BF16), w1_ref[...]) + _dot(mem_ref[...].astype(BF16), w2_ref[...]) + bias_ref[...]
    o_ref[...] = _layer_norm(alpha * x_ref[...] + y, g_ref[...], b_ref[...])


def _outproj_ln(mix, mem, w1, w2, bias, x, g, b, alpha, tm):
    m, d = x.shape
    row = lambda i: (i, 0)
    fixed = lambda i: (0, 0)
    return pl.pallas_call(
        functools.partial(_outproj_kernel, alpha),
        grid=(m // tm,),
        in_specs=[pl.BlockSpec((tm, Q_W), row), pl.BlockSpec((tm, QM_W), row),
                  pl.BlockSpec((Q_W, d), fixed), pl.BlockSpec((QM_W, d), fixed), pl.BlockSpec((1, d), fixed),
                  pl.BlockSpec((tm, d), row), pl.BlockSpec((1, d), fixed), pl.BlockSpec((1, d), fixed)],
        out_specs=pl.BlockSpec((tm, d), row),
        out_shape=jax.ShapeDtypeStruct((m, d), F32),
        compiler_params=_params(("parallel",), 48),
        name="outproj_ln",
    )(mix, mem, w1, w2, bias, x, g, b)


def _flash_init(rows):
    return (jnp.full((rows, 1), NEG_INF, F32), jnp.zeros((rows, 1), F32), jnp.zeros((rows, HEAD_DIM), F32))


def _flash_tile(carry, q, k, v, slope_col, qpos_col, kpos_row, window=None, extra=None):
    m, l, acc = carry
    dist = qpos_col - kpos_row
    s = _dot_nt(q, k) * SCALE - slope_col * dist.astype(F32)
    mask = dist >= 0
    if window is not None:
        mask = jnp.logical_and(mask, dist < window)
    if extra is not None:
        mask = jnp.logical_and(mask, extra)
    s = jnp.where(mask, s, NEG_INF)
    m_new = jnp.maximum(m, jnp.max(s, axis=1, keepdims=True))
    e = jnp.where(mask, jnp.exp(s - m_new), 0.0)
    corr = jnp.exp(m - m_new)
    l = l * corr + jnp.sum(e, axis=1, keepdims=True)
    acc = acc * corr + _dot(e.astype(BF16), v)
    return m_new, l, acc


def _flash_out(carry):
    _, l, acc = carry
    return acc / jnp.maximum(l, 1e-30)


def _flash_out_sink(carry, sink_col):
    m, l, acc = carry
    m_f = jnp.maximum(m, sink_col)
    corr = jnp.exp(m - m_f)
    return acc * corr / (l * corr + jnp.exp(sink_col - m_f))


def _stack_heads(q_ref, nq):
    return jnp.concatenate([q_ref[:, hh * HEAD_DIM:(hh + 1) * HEAD_DIM] for hh in range(HPG)], axis=0).astype(BF16)


def _head_cols(ref, base, nq):
    return jnp.concatenate([jnp.full((nq, 1), ref[base + hh], F32) for hh in range(HPG)], axis=0)


def _stacked_pos(q0, nq):
    pos = q0 + lax.broadcasted_iota(jnp.int32, (nq, 1), 0)
    return jnp.concatenate([pos] * HPG, axis=0)


def _tile_rows(x):
    return jnp.concatenate([x] * HPG, axis=0)


def _top_blocks(score, lane):
    sel = jnp.zeros_like(score)
    work = score
    for _ in range(N_SEL):
        mx = jnp.max(work, axis=1, keepdims=True)
        first = jnp.min(jnp.where(work == mx, lane, 1e9), axis=1, keepdims=True)
        hit = lane == first
        sel = jnp.where(hit, 1.0, sel)
        work = jnp.where(hit, -jnp.inf, work)
    return sel


def _cmp_branch(q_all, kc, vc, slope_col, qpos_col, ov_ref, nq, n_blk):
    ncp = kc.shape[0]
    end_row = lax.broadcasted_iota(jnp.int32, (1, ncp), 1) * CMP_STRIDE + (CMP_LEN - 1)
    carry = _flash_tile(_flash_init(q_all.shape[0]), q_all, kc, vc, slope_col, qpos_col, end_row)
    m, l, _ = carry
    o_c = _flash_out(carry)
    dist = qpos_col - end_row
    s = _dot_nt(q_all, kc) * SCALE - slope_col * dist.astype(F32)
    p = jnp.where(dist >= 0, jnp.exp(jnp.where(dist >= 0, s, NEG_INF) - m), 0.0) / jnp.maximum(l, 1e-30)
    imp_c = p[0:nq]
    for hh in range(1, HPG):
        imp_c = imp_c + p[hh * nq:(hh + 1) * nq]
    hi = imp_c.astype(BF16)
    lo = (imp_c - hi.astype(F32)).astype(BF16)
    imp = _dot(hi, ov_ref[...]) + _dot(lo, ov_ref[...])
    nl = imp.shape[1]
    lane_i = lax.broadcasted_iota(jnp.int32, (nq, nl), 1)
    cur = qpos_col[0:nq] // SEL_BLK
    forced = jnp.logical_or(lane_i == 0, jnp.logical_or(lane_i == cur, lane_i == cur - 1))
    score = jnp.where(forced, SEL_FORCE, jnp.where(lane_i <= cur, imp, -SEL_FORCE))
    score = jnp.where(lane_i < n_blk, score, -jnp.inf)
    return o_c, _top_blocks(score, lane_i.astype(F32))


def _gated_sum(gate_ref, bg_ref, branches, o_ref, nq):
    gates = jax.nn.sigmoid(gate_ref[...] + bg_ref[...])
    for hh in range(HPG):
        rows = slice(hh * nq, (hh + 1) * nq)
        o = gates[:, 3 * hh:3 * hh + 1] * branches[0][rows]
        for j in (1, 2):
            o = o + gates[:, 3 * hh + j:3 * hh + j + 1] * branches[j][rows]
        o_ref[:, hh * HEAD_DIM:(hh + 1) * HEAD_DIM] = o


def _chunk_ab(load_rows, wab_ref, c):
    acc = None
    for r in range(CMP_STRIDE):
        t = _dot(load_rows(r).astype(BF16), wab_ref[c, r])
        acc = t if acc is None else acc + t
    return acc


def _cmp_bias(pe_ref, w_ref, c):
    acc = jnp.zeros((8, HEAD_DIM), F32)
    for r in range(CMP_LEN):
        acc = acc + _dot(jnp.broadcast_to(pe_ref[c, r:r + 1, :], (8, HEAD_DIM)).astype(BF16), w_ref[c, r])
    return acc[0:1]


def _blocks_from_ab(ab, bias):
    n = ab.shape[0]
    return ab[:, :HEAD_DIM] + pltpu.roll(ab[:, HEAD_DIM:], n - 1, 0) + bias


def _compress_kernel(nch, kv_ref, wab_ref, w_ref, pe_ref, o_ref):
    for c in range(2):
        bias = _cmp_bias(pe_ref, w_ref, c)
        for g in range(N_KV):
            cg = c * N_KV + g
            ab = _chunk_ab(lambda r: kv_ref[pl.ds(r * KV_ROWS + cg, nch, stride=CMP_STRIDE * KV_ROWS), :], wab_ref, c)
            o_ref[c, g] = _blocks_from_ab(ab, bias)


def _compress_prompt(kv, wab, w, pe, batch, seq):
    nch = seq // CMP_STRIDE
    fixed4 = lambda b: (0, 0, 0, 0)
    return pl.pallas_call(
        functools.partial(_compress_kernel, nch),
        grid=(batch,),
        in_specs=[pl.BlockSpec((seq * KV_ROWS, HEAD_DIM), lambda b: (b, 0)),
                  pl.BlockSpec(wab.shape, fixed4), pl.BlockSpec(w.shape, fixed4),
                  pl.BlockSpec(pe.shape, lambda b: (0, 0, 0))],
        out_specs=pl.BlockSpec((None, 2, N_KV, nch, HEAD_DIM), lambda b: (b, 0, 0, 0, 0)),
        out_shape=jax.ShapeDtypeStruct((batch, 2, N_KV, nch, HEAD_DIM), F32),
        compiler_params=_params(("parallel",), 40),
        name="compress_prompt",
    )(kv, wab, w, pe)


def _nsa_prompt_kernel(tq, seq, slopes_ref, q_ref, gate_ref, bg_ref, kvc_ref, ks_ref, vs_ref, kw_ref, vw_ref,
                       ov_ref, e_ref, o_ref, selx_ref):
    g = pl.program_id(1)
    qi = pl.program_id(2)
    q0 = qi * tq
    q_all = _stack_heads(q_ref, tq)
    slope_col = _head_cols(slopes_ref, g * HPG, tq)
    qpos_col = _stacked_pos(q0, tq)

    o_c, sel = _cmp_branch(q_all, kvc_ref[0].astype(BF16), kvc_ref[1].astype(BF16), slope_col, qpos_col, ov_ref,
                           tq, seq // SEL_BLK)
    selx_ref[...] = _dot(sel.astype(BF16), e_ref[...])

    def key_tile(k_ref, v_ref, kj):
        k0 = pl.multiple_of(kj * tq, tq)
        kpos_row = k0 + lax.broadcasted_iota(jnp.int32, (1, tq), 1)
        return k0, k_ref[pl.ds(k0, tq), :].astype(BF16), v_ref[pl.ds(k0, tq), :].astype(BF16), kpos_row

    def sel_body(kj, carry):
        k0, k, v, kpos_row = key_tile(ks_ref, vs_ref, kj)
        extra = _tile_rows(selx_ref[:, pl.ds(k0, tq)]) > 0.5
        return _flash_tile(carry, q_all, k, v, slope_col, qpos_col, kpos_row, extra=extra)

    def win_body(kj, carry):
        _, k, v, kpos_row = key_tile(kw_ref, vw_ref, kj)
        return _flash_tile(carry, q_all, k, v, slope_col, qpos_col, kpos_row, window=WIN_A)

    rows = HPG * tq
    o_s = _flash_out(lax.fori_loop(0, qi + 1, sel_body, _flash_init(rows)))
    o_w = _flash_out(lax.fori_loop(jnp.maximum(qi - WIN_A // tq, 0), qi + 1, win_body, _flash_init(rows)))
    _gated_sum(gate_ref, bg_ref, (o_c, o_s, o_w), o_ref, tq)


def _nsa_prompt(h, kvc, slopes, bg, ov, expand, batch, seq, tq):
    nq = seq // tq
    nch = kvc.shape[3]
    kv_spec = lambda col: pl.BlockSpec((seq, HEAD_DIM), lambda b, g, i, col=col: (b, col // HEAD_DIM + g))
    return pl.pallas_call(
        functools.partial(_nsa_prompt_kernel, tq, seq),
        grid=(batch, N_KV, nq),
        in_specs=[pl.BlockSpec(memory_space=pltpu.SMEM),
                  pl.BlockSpec((tq, HPG * HEAD_DIM), lambda b, g, i: (b * nq + i, g)),
                  pl.BlockSpec((tq, LANES), lambda b, g, i: (b * nq + i, A_GATE // LANES + g)),
                  pl.BlockSpec((1, LANES), lambda b, g, i: (0, g)),
                  pl.BlockSpec((None, 2, None, nch, HEAD_DIM), lambda b, g, i: (b, 0, g, 0, 0)),
                  kv_spec(A_KS), kv_spec(A_KS + N_KV * HEAD_DIM), kv_spec(A_KW), kv_spec(A_KW + N_KV * HEAD_DIM),
                  pl.BlockSpec(ov.shape, lambda b, g, i: (0, 0)),
                  pl.BlockSpec(expand.shape, lambda b, g, i: (0, 0))],
        out_specs=pl.BlockSpec((tq, HPG * HEAD_DIM), lambda b, g, i: (b * nq + i, g)),
        out_shape=jax.ShapeDtypeStruct((batch * seq, Q_W), F32),
        scratch_shapes=[pltpu.VMEM((tq, seq), F32)],
        compiler_params=_params(("parallel", "parallel", "arbitrary"), 48),
        name="nsa_prompt",
    )(slopes, h, h, bg, kvc, h, h, h, h, ov, expand)


def _swa_prompt_kernel(tq, slopes_ref, sinks_ref, q_ref, k_ref, v_ref, o_ref):
    g = pl.program_id(1)
    qi = pl.program_id(2)
    q_all = _stack_heads(q_ref, tq)
    slope_col = _head_cols(slopes_ref, g * HPG, tq)
    sink_col = _head_cols(sinks_ref, g * HPG, tq)
    qpos_col = _stacked_pos(qi * tq, tq)

    def body(kj, carry):
        k0 = pl.multiple_of(kj * tq, tq)
        kpos_row = k0 + lax.broadcasted_iota(jnp.int32, (1, tq), 1)
        k = k_ref[pl.ds(k0, tq), :].astype(BF16)
        v = v_ref[pl.ds(k0, tq), :].astype(BF16)
        return _flash_tile(carry, q_all, k, v, slope_col, qpos_col, kpos_row, window=WIN_B)

    lo = jnp.maximum(qi - (WIN_B + tq - 1) // tq, 0)
    o = _flash_out_sink(lax.fori_loop(lo, qi + 1, body, _flash_init(HPG * tq)), sink_col)
    for hh in range(HPG):
        o_ref[:, hh * HEAD_DIM:(hh + 1) * HEAD_DIM] = o[hh * tq:(hh + 1) * tq]


def _swa_prompt(h, slopes, sinks, batch, seq, tq):
    nq = seq // tq
    kv_spec = lambda col: pl.BlockSpec((seq, HEAD_DIM), lambda b, g, i, col=col: (b, col // HEAD_DIM + g))
    smem = pl.BlockSpec(memory_space=pltpu.SMEM)
    return pl.pallas_call(
        functools.partial(_swa_prompt_kernel, tq),
        grid=(batch, N_KV, nq),
        in_specs=[smem, smem,
                  pl.BlockSpec((tq, HPG * HEAD_DIM), lambda b, g, i: (b * nq + i, g)),
                  kv_spec(B_KV), kv_spec(B_KV + N_KV * HEAD_DIM)],
        out_specs=pl.BlockSpec((tq, HPG * HEAD_DIM), lambda b, g, i: (b * nq + i, g)),
        out_shape=jax.ShapeDtypeStruct((batch * seq, Q_W), F32),
        compiler_params=_params(("parallel", "parallel", "arbitrary"), 40),
        name="swa_prompt",
    )(slopes, sinks, h, h, h)


def _mem_kernel(q_ref, kv_ref, o_ref):
    for hd in range(N_MEM_HEADS):
        cols = slice(hd * HEAD_DIM, (hd + 1) * HEAD_DIM)
        k = kv_ref[:, hd * HEAD_DIM:(hd + 1) * HEAD_DIM].astype(BF16)
        v = kv_ref[:, QM_W + hd * HEAD_DIM:QM_W + (hd + 1) * HEAD_DIM].astype(BF16)
        s = _dot_nt(q_ref[:, cols].astype(BF16), k) * SCALE
        e = jnp.exp(s - jnp.max(s, axis=1, keepdims=True))
        p = e / jnp.sum(e, axis=1, keepdims=True)
        o_ref[:, cols] = _dot(p.astype(BF16), v)


def _mem_attend(h, qm_col, row0, mkv, tq, blocks_per_batch):
    n_batch, n_mem, _ = mkv.shape
    nblk = n_batch * blocks_per_batch
    return pl.pallas_call(
        _mem_kernel,
        grid=(nblk,),
        in_specs=[pl.BlockSpec((tq, QM_W), lambda i: (row0 // tq + i, qm_col // QM_W)),
                  pl.BlockSpec((None, n_mem, 2 * QM_W), lambda i: (i // blocks_per_batch, 0, 0))],
        out_specs=pl.BlockSpec((tq, QM_W), lambda i: (i, 0)),
        out_shape=jax.ShapeDtypeStruct((nblk * tq, QM_W), F32),
        compiler_params=_params(("parallel",), 32),
        name="mem_attend",
    )(h, mkv)


def _page_rows(page_refs, r, cg):
    return jnp.concatenate([p[pl.ds(r * KV_ROWS + cg, PAGE_SIZE // CMP_STRIDE, stride=CMP_STRIDE * KV_ROWS), :]
                            for p in page_refs], axis=0)


def _nsa_sample_cmp_kernel(t_new, past, *refs):
    npg = PAGES_PER_STEP
    _, slopes_ref = refs[0], refs[1]
    page_refs = refs[2:2 + npg]
    q_ref, wab_ref, w_ref, pe_ref, ov_ref, oc_ref, sel_ref, ab_ref = refs[2 + npg:]
    j = pl.program_id(1)
    rows_per_step = npg * PAGE_SIZE // CMP_STRIDE
    for c in range(2):
        for g in range(N_KV):
            cg = c * N_KV + g
            ab = _chunk_ab(lambda r: _page_rows(page_refs, r, cg), wab_ref, c)
            ab_ref[cg, pl.ds(pl.multiple_of(j * rows_per_step, rows_per_step), rows_per_step), :] = ab

    @pl.when(j == pl.num_programs(1) - 1)
    def _():
        bias = [_cmp_bias(pe_ref, w_ref, c) for c in range(2)]
        n_blk = -(-(past + t_new) // SEL_BLK)
        for g in range(N_KV):
            kc = _blocks_from_ab(ab_ref[g], bias[0]).astype(BF16)
            vc = _blocks_from_ab(ab_ref[N_KV + g], bias[1]).astype(BF16)
            q_all = _stack_heads(q_ref.at[:, g * HPG * HEAD_DIM:(g + 1) * HPG * HEAD_DIM], t_new)
            slope_col = _head_cols(slopes_ref, g * HPG, t_new)
            qpos_col = _stacked_pos(past, t_new)
            o_c, sel = _cmp_branch(q_all, kc, vc, slope_col, qpos_col, ov_ref, t_new, n_blk)
            for hh in range(HPG):
                oc_ref[:, (g * HPG + hh) * HEAD_DIM:(g * HPG + hh + 1) * HEAD_DIM] = o_c[hh * t_new:(hh + 1) * t_new]
            sel_ref[g] = sel


def _nsa_sample_cmp(h, row0, cache, lk, page_table, slopes, wab, w, pe, ov, n_seq, t_new, past):
    n_pages = past // PAGE_SIZE
    npg = PAGES_PER_STEP
    nch = past // CMP_STRIDE
    page_spec = lambda k: pl.BlockSpec((None, None, PAGE_SIZE * KV_ROWS, HEAD_DIM),
                                       lambda b, j, pt, k=k: (lk, pt[b, j * npg + k], 0, 0))
    fixed = lambda nd: (lambda b, j, pt: (0,) * nd)
    grid_spec = pltpu.PrefetchScalarGridSpec(
        num_scalar_prefetch=1,
        grid=(n_seq, n_pages // npg),
        in_specs=[pl.BlockSpec(memory_space=pltpu.SMEM)] + [page_spec(k) for k in range(npg)] + [
            pl.BlockSpec((t_new, Q_W), lambda b, j, pt: (row0 // t_new + b, 0)),
            pl.BlockSpec(wab.shape, fixed(4)), pl.BlockSpec(w.shape, fixed(4)), pl.BlockSpec(pe.shape, fixed(3)),
            pl.BlockSpec(ov.shape, fixed(2))],
        out_specs=[pl.BlockSpec((t_new, Q_W), lambda b, j, pt: (b, 0)),
                   pl.BlockSpec((None, N_KV, t_new, ov.shape[1]), lambda b, j, pt: (b, 0, 0, 0))],
        scratch_shapes=[pltpu.VMEM((2 * N_KV, nch, 2 * HEAD_DIM), F32)],
    )
    return pl.pallas_call(
        functools.partial(_nsa_sample_cmp_kernel, t_new, past),
        grid_spec=grid_spec,
        out_shape=[jax.ShapeDtypeStruct((n_seq * t_new, Q_W), F32),
                   jax.ShapeDtypeStruct((n_seq, N_KV, t_new, ov.shape[1]), F32)],
        compiler_params=_params(("parallel", "arbitrary"), 48),
        name="nsa_sample_cmp",
    )(page_table, slopes, *([cache] * npg), h, wab, w, pe, ov)


def _pad_keys(new_ref, col, t_new):
    rows = new_ref[:, col:col + HEAD_DIM]
    return jnp.concatenate([rows, jnp.zeros((LANES - t_new, HEAD_DIM), F32)], axis=0).astype(BF16)


def _nsa_sample_sel_kernel(t_new, past, *refs):
    npg = PAGES_PER_STEP
    slopes_ref = refs[1]
    page_refs = refs[2:2 + npg]
    (q_ref, sel_ref, e_ref, oc_ref, ks_new_ref, kw_new_ref, gate_ref, bg_ref, win_ref,
     o_ref, m_ref, l_ref, acc_ref) = refs[2 + npg:]
    j = pl.program_id(1)
    rows = HPG * t_new
    keys = npg * PAGE_SIZE
    qpos_col = _stacked_pos(past, t_new)

    @pl.when(j == 0)
    def _():
        m_ref[...] = jnp.full(m_ref.shape, NEG_INF, F32)
        l_ref[...] = jnp.zeros(l_ref.shape, F32)
        acc_ref[...] = jnp.zeros(acc_ref.shape, F32)

    kpos_row = j * keys + lax.broadcasted_iota(jnp.int32, (1, keys), 1)
    for g in range(N_KV):
        q_all = _stack_heads(q_ref.at[:, g * HPG * HEAD_DIM:(g + 1) * HPG * HEAD_DIM], t_new)
        slope_col = _head_cols(slopes_ref, g * HPG, t_new)
        k = jnp.concatenate([p[:, g * HEAD_DIM:(g + 1) * HEAD_DIM] for p in page_refs], axis=0).astype(BF16)
        v = jnp.concatenate([p[:, (N_KV + g) * HEAD_DIM:(N_KV + g + 1) * HEAD_DIM] for p in page_refs],
                            axis=0).astype(BF16)
        extra = _tile_rows(_dot(sel_ref[g].astype(BF16), e_ref[...])) > 0.5
        carry = _flash_tile((m_ref[g], l_ref[g], acc_ref[g]), q_all, k, v, slope_col, qpos_col, kpos_row, extra=extra)
        m_ref[g], l_ref[g], acc_ref[g] = carry

    @pl.when(j == pl.num_programs(1) - 1)
    def _():
        new_pos = past + lax.broadcasted_iota(jnp.int32, (1, LANES), 1)
        wb = win_ref.shape[0]
        win_pos = past - wb + lax.broadcasted_iota(jnp.int32, (1, wb), 1)
        for g in range(N_KV):
            q_all = _stack_heads(q_ref.at[:, g * HPG * HEAD_DIM:(g + 1) * HPG * HEAD_DIM], t_new)
            slope_col = _head_cols(slopes_ref, g * HPG, t_new)
            sel_new = _tile_rows(sel_ref[g][:, past // SEL_BLK:past // SEL_BLK + 1]) > 0.5
            carry = _flash_tile((m_ref[g], l_ref[g], acc_ref[g]), q_all,
                                _pad_keys(ks_new_ref, g * HEAD_DIM, t_new),
                                _pad_keys(ks_new_ref, (N_KV + g) * HEAD_DIM, t_new),
                                slope_col, qpos_col, new_pos, extra=sel_new)
            o_s = _flash_out(carry)
            carry = _flash_tile(_flash_init(rows), q_all,
                                win_ref[:, g * HEAD_DIM:(g + 1) * HEAD_DIM].astype(BF16),
                                win_ref[:, (N_KV + g) * HEAD_DIM:(N_KV + g + 1) * HEAD_DIM].astype(BF16),
                                slope_col, qpos_col, win_pos, window=WIN_A)
            carry = _flash_tile(carry, q_all, _pad_keys(kw_new_ref, g * HEAD_DIM, t_new),
                                _pad_keys(kw_new_ref, (N_KV + g) * HEAD_DIM, t_new),
                                slope_col, qpos_col, new_pos, window=WIN_A)
            o_w = _flash_out(carry)
            o_c = jnp.concatenate([oc_ref[:, (g * HPG + hh) * HEAD_DIM:(g * HPG + hh + 1) * HEAD_DIM]
                                   for hh in range(HPG)], axis=0)
            _gated_sum(gate_ref.at[:, g * LANES:(g + 1) * LANES], bg_ref.at[:, g * LANES:(g + 1) * LANES],
                       (o_c, o_s, o_w), o_ref.at[:, g * HPG * HEAD_DIM:(g + 1) * HPG * HEAD_DIM], t_new)


def _nsa_sample_sel(h, row0, cache, lk, page_table, slopes, sel, expand, oc, bg, win, n_seq, t_new, past):
    n_pages = past // PAGE_SIZE
    npg = PAGES_PER_STEP
    nbl = sel.shape[3]
    wb = win.shape[1]
    rows = HPG * t_new
    page_spec = lambda k: pl.BlockSpec((None, None, PAGE_SIZE, KV_W),
                                       lambda b, j, pt, k=k: (lk, pt[b, j * npg + k], 0, 0))
    hrow = lambda width, col: pl.BlockSpec((t_new, width), lambda b, j, pt: (row0 // t_new + b, col // width))
    grid_spec = pltpu.PrefetchScalarGridSpec(
        num_scalar_prefetch=1,
        grid=(n_seq, n_pages // npg),
        in_specs=[pl.BlockSpec(memory_space=pltpu.SMEM)] + [page_spec(k) for k in range(npg)] + [
            hrow(Q_W, 0),
            pl.BlockSpec((None, N_KV, t_new, nbl), lambda b, j, pt: (b, 0, 0, 0)),
            pl.BlockSpec((nbl, npg * PAGE_SIZE), lambda b, j, pt: (0, j)),
            pl.BlockSpec((t_new, Q_W), lambda b, j, pt: (b, 0)),
            hrow(KV_W, A_KS), hrow(KV_W, A_KW), hrow(N_KV * LANES, A_GATE),
            pl.BlockSpec((1, N_KV * LANES), lambda b, j, pt: (0, 0)),
            pl.BlockSpec((None, wb, KV_W), lambda b, j, pt: (b, 0, 0))],
        out_specs=pl.BlockSpec((t_new, Q_W), lambda b, j, pt: (b, 0)),
        scratch_shapes=[pltpu.VMEM((N_KV, rows, 1), F32), pltpu.VMEM((N_KV, rows, 1), F32),
                        pltpu.VMEM((N_KV, rows, HEAD_DIM), F32)],
    )
    return pl.pallas_call(
        functools.partial(_nsa_sample_sel_kernel, t_new, past),
        grid_spec=grid_spec,
        out_shape=jax.ShapeDtypeStruct((n_seq * t_new, Q_W), F32),
        compiler_params=_params(("parallel", "arbitrary"), 48),
        name="nsa_sample_sel",
    )(page_table, slopes, *([cache] * npg), h, sel, expand, oc, h, h, h, bg, win)


def _swa_sample_kernel(t_new, past, slopes_ref, sinks_ref, q_ref, new_ref, buf_ref, o_ref):
    wb = buf_ref.shape[0]
    qpos_col = _stacked_pos(past, t_new)
    buf_pos = past - wb + lax.broadcasted_iota(jnp.int32, (1, wb), 1)
    new_pos = past + lax.broadcasted_iota(jnp.int32, (1, LANES), 1)
    for g in range(N_KV):
        q_all = _stack_heads(q_ref.at[:, g * HPG * HEAD_DIM:(g + 1) * HPG * HEAD_DIM], t_new)
        slope_col = _head_cols(slopes_ref, g * HPG, t_new)
        sink_col = _head_cols(sinks_ref, g * HPG, t_new)
        carry = _flash_tile(_flash_init(HPG * t_new), q_all,
                            buf_ref[:, g * HEAD_DIM:(g + 1) * HEAD_DIM].astype(BF16),
                            buf_ref[:, (N_KV + g) * HEAD_DIM:(N_KV + g + 1) * HEAD_DIM].astype(BF16),
                            slope_col, qpos_col, buf_pos, window=WIN_B)
        carry = _flash_tile(carry, q_all, _pad_keys(new_ref, g * HEAD_DIM, t_new),
                            _pad_keys(new_ref, (N_KV + g) * HEAD_DIM, t_new),
                            slope_col, qpos_col, new_pos, window=WIN_B)
        o = _flash_out_sink(carry, sink_col)
        for hh in range(HPG):
            o_ref[:, (g * HPG + hh) * HEAD_DIM:(g * HPG + hh + 1) * HEAD_DIM] = o[hh * t_new:(hh + 1) * t_new]


def _swa_sample(h, row0, buf, slopes, sinks, n_seq, t_new, past):
    wb = buf.shape[1]
    smem = pl.BlockSpec(memory_space=pltpu.SMEM)
    return pl.pallas_call(
        functools.partial(_swa_sample_kernel, t_new, past),
        grid=(n_seq,),
        in_specs=[smem, smem,
                  pl.BlockSpec((t_new, Q_W), lambda b: (row0 // t_new + b, 0)),
                  pl.BlockSpec((t_new, KV_W), lambda b: (row0 // t_new + b, B_KV // KV_W)),
                  pl.BlockSpec((None, wb, KV_W), lambda b: (b, 0, 0))],
        out_specs=pl.BlockSpec((t_new, Q_W), lambda b: (b, 0)),
        out_shape=jax.ShapeDtypeStruct((n_seq * t_new, Q_W), F32),
        compiler_params=_params(("parallel",), 32),
        name="swa_sample",
    )(slopes, sinks, h, h, buf)


def _top_values(work, out_ref):
    for k in range(PEER_TOPK):
        mx = jnp.max(work, axis=0, keepdims=True)
        out_ref[k:k + 1, :] = mx
        work = jnp.where(work == mx, -jnp.inf, work)


def _peer_route_kernel(x_ref, wq_ref, sk_ref, a_ref, b_ref, ea_ref, eb_ref, thr_ref, ta_ref, tb_ref, tc_ref):
    q = _dot(x_ref[...].astype(BF16), wq_ref[...])
    for hd in range(PEER_HEADS):
        halves = []
        for c in range(2):
            col = (2 * hd + c) * PEER_HALF
            halves.append(_dot_nt(sk_ref[2 * hd + c], q[:, col:col + PEER_HALF].astype(BF16)))
        s_a, s_b = halves
        _top_values(s_a, ta_ref)
        _top_values(s_b, tb_ref)
        top_b = tb_ref[...]
        cand = jnp.concatenate([ta_ref[k:k + 1, :] + top_b for k in range(PEER_TOPK)], axis=0)
        _top_values(cand, tc_ref)
        sc = tc_ref[...]
        z = jnp.sum(jnp.exp(sc - sc[0:1]), axis=0, keepdims=True)
        a_ref[hd] = s_a
        b_ref[hd] = s_b
        ea_ref[hd] = jnp.exp(s_a - ta_ref[0:1, :]) / z
        eb_ref[hd] = jnp.exp(s_b - tb_ref[0:1, :])
        thr_ref[hd:hd + 1, :] = sc[PEER_TOPK - 1:PEER_TOPK]


def _peer_route(x, wq, sk, tb):
    n, d = x.shape
    big = jax.ShapeDtypeStruct((PEER_HEADS, PEER_KEYS, n), F32)
    big_spec = pl.BlockSpec((PEER_HEADS, PEER_KEYS, tb), lambda i: (0, 0, i))
    top = pltpu.VMEM((PEER_TOPK, tb), F32)
    return pl.pallas_call(
        _peer_route_kernel,
        grid=(n // tb,),
        in_specs=[pl.BlockSpec((tb, d), lambda i: (i, 0)), pl.BlockSpec(wq.shape, lambda i: (0, 0)),
                  pl.BlockSpec(sk.shape, lambda i: (0, 0, 0))],
        out_specs=[big_spec] * 4 + [pl.BlockSpec((PEER_HEADS, tb), lambda i: (0, i))],
        out_shape=[big] * 4 + [jax.ShapeDtypeStruct((PEER_HEADS, n), F32)],
        scratch_shapes=[top, top, top],
        compiler_params=_params(("parallel",), 48),
        name="peer_route",
    )(x, wq, sk)


def _peer_expert_kernel(alpha, ec, x_ref, a_ref, b_ref, ea_ref, eb_ref, thr_ref, u_ref, v_ref, g_ref, beta_ref,
                        o_ref, acc_ref, xb_ref, hid_ref):
    j = pl.program_id(1)

    @pl.when(j == 0)
    def _():
        acc_ref[...] = jnp.zeros(acc_ref.shape, F32)
        xb_ref[...] = x_ref[...].astype(BF16)

    act = _dot_nt(u_ref[...], xb_ref[...])
    gelu = 0.5 * act * (1.0 + lax.erf(act * SQRT_HALF))
    per_step = ec // PEER_KEYS
    for ii in range(per_step):
        i = j * per_step + ii
        w = jnp.zeros((PEER_KEYS, act.shape[1]), F32)
        for hd in range(PEER_HEADS):
            t = a_ref[hd, pl.ds(i, 1), :] + b_ref[hd]
            w = w + jnp.where(t >= thr_ref[hd:hd + 1, :], eb_ref[hd], 0.0) * ea_ref[hd, pl.ds(i, 1), :]
        hid_ref[ii * PEER_KEYS:(ii + 1) * PEER_KEYS, :] = gelu[ii * PEER_KEYS:(ii + 1) * PEER_KEYS] * w
    acc_ref[...] += _dot(hid_ref[...].T.astype(BF16), v_ref[...])

    @pl.when(j == pl.num_programs(1) - 1)
    def _():
        o_ref[...] = _layer_norm(alpha * x_ref[...] + acc_ref[...], g_ref[...], beta_ref[...])


def _peer_experts(x, route, u, v, g, beta, alpha, tb, ec):
    n, d = x.shape
    ne = u.shape[0]
    a, b, ea, eb, thr = route
    big_spec = pl.BlockSpec((PEER_HEADS, PEER_KEYS, tb), lambda i, j: (0, 0, i))
    return pl.pallas_call(
        functools.partial(_peer_expert_kernel, alpha, ec),
        grid=(n // tb, ne // ec),
        in_specs=[pl.BlockSpec((tb, d), lambda i, j: (i, 0))] + [big_spec] * 4 + [
            pl.BlockSpec((PEER_HEADS, tb), lambda i, j: (0, i)),
            pl.BlockSpec((ec, d), lambda i, j: (j, 0)), pl.BlockSpec((ec, d), lambda i, j: (j, 0)),
            pl.BlockSpec((1, d), lambda i, j: (0, 0)), pl.BlockSpec((1, d), lambda i, j: (0, 0))],
        out_specs=pl.BlockSpec((tb, d), lambda i, j: (i, 0)),
        out_shape=jax.ShapeDtypeStruct((n, d), F32),
        scratch_shapes=[pltpu.VMEM((tb, d), F32), pltpu.VMEM((tb, d), BF16), pltpu.VMEM((ec, tb), F32)],
        compiler_params=_params(("parallel", "arbitrary"), 48),
        name="peer_experts",
    )(x, a, b, ea, eb, thr, u, v, g, beta)


def _overlap_table(n_cmp_pad, n_blk, n_lanes):
    c0 = jnp.arange(n_cmp_pad)[:, None] * CMP_STRIDE
    s0 = jnp.arange(n_lanes)[None, :] * SEL_BLK
    ov = (c0 <= s0 + SEL_BLK - 1) & (c0 + CMP_LEN - 1 >= s0) & (jnp.arange(n_lanes)[None, :] < n_blk)
    return ov.astype(BF16)


def _expand_table(n_lanes, n_pos):
    return (jnp.arange(n_lanes)[:, None] == jnp.arange(n_pos)[None, :] // SEL_BLK).astype(BF16)


def _alibi_slopes():
    i = jnp.arange(1, N_HEADS + 1, dtype=F32)
    return jnp.exp2(-8.0 * i / N_HEADS)


def _round_up(x, m):
    return -(-x // m) * m


def kernel(x_prompt, x_sample, mem_prompt, cache_a_cmp_kv, cache_a_sel_kv, cache_a_win_kv, cache_b_kv, cache_mem_kv, page_table, w_in_a, b_gate_a, w_cmp_a, pe_cmp_a, w_in_b, sinks_b, w_mem_kv, w_out, b_out, ln_g, ln_b, peer_wq, peer_subkeys, peer_u, peer_v):
    batch, seq, d = x_prompt.shape
    n_seq, t_new, _ = x_sample.shape
    depth = w_out.shape[0]
    past = page_table.shape[1] * PAGE_SIZE
    n_prompt = batch * seq
    n_tok = n_prompt + n_seq * t_new
    alpha = (2.0 * depth) ** 0.25
    tq = min(256, seq)
    tm = 256
    slopes = _alibi_slopes()

    x = jnp.concatenate([x_prompt.reshape(n_prompt, d), x_sample.reshape(n_seq * t_new, d)], axis=0)
    kv_shape = lambda n, t, g: (n, t, 2, g, HEAD_DIM)
    a_cmp_p, a_cmp_s, a_sel_p, a_sel_s, a_win_p, a_win_s, b_kv_p, b_kv_s, mem_p = ([] for _ in range(9))

    for li in range(depth):
        lk = li // 2
        mkv_p = _matmul(mem_prompt.reshape(-1, d), w_mem_kv[li].astype(BF16), tm, 2 * QM_W)
        mkv_p = mkv_p.reshape(batch, -1, 2 * QM_W)
        mem_p.append(mkv_p.reshape(batch, -1, 2, N_MEM_HEADS, HEAD_DIM))
        mkv_s = cache_mem_kv[li].reshape(n_seq, -1, 2 * QM_W)

        if li % 2 == 0:
            w_in = w_in_a[lk]
            g0 = Q_W + 3 * KV_W
            gate_w = w_in[:, g0:g0 + 3 * N_HEADS].reshape(d, N_KV, 3 * HPG)
            gate_w = jnp.pad(gate_w, ((0, 0), (0, 0), (0, LANES - 3 * HPG))).reshape(d, N_KV * LANES)
            w_r = jnp.concatenate([w_in[:, :g0], w_in[:, g0 + 3 * N_HEADS:], gate_w], axis=1).astype(BF16)
            bg = jnp.pad(b_gate_a[lk].reshape(N_KV, 3 * HPG), ((0, 0), (0, LANES - 3 * HPG))).reshape(1, N_KV * LANES)
            h = _matmul(x, w_r, tm, A_WIDTH // 6)

            w_cmp = w_cmp_a[lk].astype(BF16)
            wab = jnp.concatenate([w_cmp[:, :CMP_STRIDE], w_cmp[:, CMP_STRIDE:]], axis=-1)
            pe = pe_cmp_a[lk]
            hp = h[:n_prompt].reshape(batch, seq, -1)
            hs = h[n_prompt:].reshape(n_seq, t_new, -1)
            a_cmp_p.append(hp[..., A_KC:A_KC + KV_W].reshape(kv_shape(batch, seq, N_KV)))
            kvc_p = _compress_prompt(a_cmp_p[-1].reshape(-1, HEAD_DIM), wab, w_cmp, pe, batch, seq)
            n_blk_p = seq // SEL_BLK
            ov_p = _overlap_table(seq // CMP_STRIDE, n_blk_p, _round_up(n_blk_p, LANES))
            ex_p = _expand_table(_round_up(n_blk_p, LANES), seq)
            mix_p = _nsa_prompt(h, kvc_p, slopes, bg, ov_p, ex_p, batch, seq, tq)

            n_blk_s = -(-(past + t_new) // SEL_BLK)
            nbl = _round_up(n_blk_s, LANES)
            ov_s = _overlap_table(past // CMP_STRIDE, n_blk_s, nbl)
            ex_s = _expand_table(nbl, past)
            n_pool = cache_a_cmp_kv.shape[1]
            cmp_pages = cache_a_cmp_kv.reshape(-1, n_pool, PAGE_SIZE * KV_ROWS, HEAD_DIM)
            sel_pages = cache_a_sel_kv.reshape(-1, n_pool, PAGE_SIZE, KV_W)
            win = cache_a_win_kv[lk].reshape(n_seq, -1, KV_W)
            oc_s, sel_s = _nsa_sample_cmp(h, n_prompt, cmp_pages, lk, page_table, slopes, wab, w_cmp, pe, ov_s,
                                          n_seq, t_new, past)
            mix_s = _nsa_sample_sel(h, n_prompt, sel_pages, lk, page_table, slopes, sel_s, ex_s, oc_s, bg, win,
                                    n_seq, t_new, past)

            a_cmp_s.append(hs[..., A_KC:A_KC + KV_W].reshape(kv_shape(n_seq, t_new, N_KV)))
            a_sel_p.append(hp[..., A_KS:A_KS + KV_W].reshape(kv_shape(batch, seq, N_KV)))
            a_sel_s.append(hs[..., A_KS:A_KS + KV_W].reshape(kv_shape(n_seq, t_new, N_KV)))
            wn = min(WIN_A, seq)
            a_win_p.append(hp[:, seq - wn:, A_KW:A_KW + KV_W].reshape(kv_shape(batch, wn, N_KV)))
            wb = win.shape[1]
            all_w = jnp.concatenate([win, hs[..., A_KW:A_KW + KV_W]], axis=1)
            a_win_s.append(all_w[:, all_w.shape[1] - wb:].reshape(kv_shape(n_seq, wb, N_KV)))
            qm_col = A_QM
        else:
            h = _matmul(x, w_in_b[lk].astype(BF16), tm, B_WIDTH // 4)
            buf = cache_b_kv[lk].reshape(n_seq, -1, KV_W)
            mix_p = _swa_prompt(h, slopes, sinks_b[lk], batch, seq, tq)
            mix_s = _swa_sample(h, n_prompt, buf, slopes, sinks_b[lk], n_seq, t_new, past)
            hp = h[:n_prompt].reshape(batch, seq, -1)
            hs = h[n_prompt:].reshape(n_seq, t_new, -1)
            wn = min(WIN_B, seq)
            b_kv_p.append(hp[:, seq - wn:, B_KV:B_KV + KV_W].reshape(kv_shape(batch, wn, N_KV)))
            wb = buf.shape[1]
            all_kv = jnp.concatenate([buf, hs[..., B_KV:B_KV + KV_W]], axis=1)
            b_kv_s.append(all_kv[:, all_kv.shape[1] - wb:].reshape(kv_shape(n_seq, wb, N_KV)))
            qm_col = B_QM

        mem_o_p = _mem_attend(h, qm_col, 0, mkv_p, tq, seq // tq)
        mem_o_s = _mem_attend(h, qm_col, n_prompt, mkv_s, t_new, 1)
        mix = jnp.concatenate([mix_p, mix_s], axis=0)
        mem_o = jnp.concatenate([mem_o_p, mem_o_s], axis=0)
        w_o = w_out[li].astype(BF16)
        x = _outproj_ln(mix, mem_o, w_o[:Q_W], w_o[Q_W:], b_out[li].reshape(1, d), x,
                        ln_g[li, 0].reshape(1, d), ln_b[li, 0].reshape(1, d), alpha, tm)

        sk = peer_subkeys[li].reshape(2 * PEER_HEADS, PEER_KEYS, PEER_HALF).astype(BF16)
        route = _peer_route(x, peer_wq[li].astype(BF16), sk, 256)
        x = _peer_experts(x, route, peer_u[li].astype(BF16), peer_v[li].astype(BF16),
                          ln_g[li, 1].reshape(1, d), ln_b[li, 1].reshape(1, d), alpha, 256, 512)

    return (x[:n_prompt].reshape(batch, seq, d), x[n_prompt:].reshape(n_seq, t_new, d),
            jnp.stack(a_cmp_p), jnp.stack(a_cmp_s), jnp.stack(a_sel_p), jnp.stack(a_sel_s),
            jnp.stack(a_win_p), jnp.stack(a_win_s), jnp.stack(b_kv_p), jnp.stack(b_kv_s), jnp.stack(mem_p))
```

```python
import functools
import math

import jax
import jax.numpy as jnp
from jax import lax
from jax.experimental import pallas as pl
from jax.experimental.pallas import tpu as pltpu

F32 = jnp.float32
BF16 = jnp.bfloat16

HEAD_DIM = 128
N_MEM_HEADS = 4
N_KV = 2
HPG = 6
N_HEADS = N_KV * HPG
Q_W = N_HEADS * HEAD_DIM
KV_W = 2 * N_KV * HEAD_DIM
KV_ROWS = 2 * N_KV
QM_W = N_MEM_HEADS * HEAD_DIM
CMP_STRIDE = 16
CMP_LEN = 32
SEL_BLK = 64
N_SEL = 16
WIN_A = 512
WIN_B = 128
PAGE_SIZE = 128
PEER_HEADS = 8
PEER_KEYS = 128
PEER_TOPK = 16
PEER_HALF = 128
LN_EPS = 1e-5
NEG_INF = -1e30
SEL_FORCE = 1e4
SCALE = HEAD_DIM ** -0.5
SQRT_HALF = math.sqrt(0.5)
LANES = 128

TM_PROJ = 1024
TM_OUT = 512
TQ_ATTN = 256
TB_PEER = 512
EC_PEER = 512
PAGES_PER_STEP = 16

A_KC, A_KS, A_KW, A_QM, A_GATE = Q_W, Q_W + KV_W, Q_W + 2 * KV_W, Q_W + 3 * KV_W, Q_W + 3 * KV_W + QM_W
A_WIDTH = A_GATE + N_KV * LANES
B_KV, B_QM = Q_W, Q_W + KV_W
B_WIDTH = B_QM + QM_W


def _params(sem, vmem_mb):
    return pltpu.CompilerParams(dimension_semantics=sem, vmem_limit_bytes=vmem_mb << 20)


def _dot(a, b):
    return jnp.dot(a, b, preferred_element_type=F32)


def _dot_nt(a, b):
    return lax.dot_general(a, b, (((1,), (1,)), ((), ())), preferred_element_type=F32)


def _mm_kernel(x_ref, w_ref, o_ref):
    o_ref[...] = _dot(x_ref[...].astype(BF16), w_ref[...])


def _matmul(x, w, tm, tn):
    m, k = x.shape
    n = w.shape[1]
    return pl.pallas_call(
        _mm_kernel,
        grid=(m // tm, n // tn),
        in_specs=[pl.BlockSpec((tm, k), lambda i, j: (i, 0)), pl.BlockSpec((k, tn), lambda i, j: (0, j))],
        out_specs=pl.BlockSpec((tm, tn), lambda i, j: (i, j)),
        out_shape=jax.ShapeDtypeStruct((m, n), F32),
        compiler_params=_params(("parallel", "parallel"), 40),
        name="proj",
    )(x, w)


def _layer_norm(z, g, b):
    mu = jnp.mean(z, axis=-1, keepdims=True)
    zc = z - mu
    var = jnp.mean(zc * zc, axis=-1, keepdims=True)
    return zc * lax.rsqrt(var + LN_EPS) * g + b


def _outproj_kernel(alpha, mix_ref, mem_ref, w1_ref, w2_ref, bias_ref, x_ref, g_ref, b_ref, o_ref):
    y = _dot(mix_ref[...].astype(BF16), w1_ref[...]) + _dot(mem_ref[...].astype(BF16), w2_ref[...]) + bias_ref[...]
    o_ref[...] = _layer_norm(alpha * x_ref[...] + y, g_ref[...], b_ref[...])


def _outproj_ln(mix, mem, w1, w2, bias, x, g, b, alpha, tm):
    m, d = x.shape
    row = lambda i: (i, 0)
    fixed = lambda i: (0, 0)
    return pl.pallas_call(
        functools.partial(_outproj_kernel, alpha),
        grid=(m // tm,),
        in_specs=[pl.BlockSpec((tm, Q_W), row), pl.BlockSpec((tm, QM_W), row),
                  pl.BlockSpec((Q_W, d), fixed), pl.BlockSpec((QM_W, d), fixed), pl.BlockSpec((1, d), fixed),
                  pl.BlockSpec((tm, d), row), pl.BlockSpec((1, d), fixed), pl.BlockSpec((1, d), fixed)],
        out_specs=pl.BlockSpec((tm, d), row),
        out_shape=jax.ShapeDtypeStruct((m, d), F32),
        compiler_params=_params(("parallel",), 48),
        name="outproj_ln",
    )(mix, mem, w1, w2, bias, x, g, b)


def _flash_init(rows):
    return (jnp.full((rows, 1), NEG_INF, F32), jnp.zeros((rows, 1), F32), jnp.zeros((rows, HEAD_DIM), F32))


def _flash_tile(carry, q, k, v, slope_col, qpos_col, kpos_row, window=None, extra=None):
    m, l, acc = carry
    dist = qpos_col - kpos_row
    s = _dot_nt(q, k) * SCALE - slope_col * dist.astype(F32)
    mask = dist >= 0
    if window is not None:
        mask = jnp.logical_and(mask, dist < window)
    if extra is not None:
        mask = jnp.logical_and(mask, extra)
    s = jnp.where(mask, s, NEG_INF)
    m_new = jnp.maximum(m, jnp.max(s, axis=1, keepdims=True))
    e = jnp.where(mask, jnp.exp(s - m_new), 0.0)
    corr = jnp.exp(m - m_new)
    l = l * corr + jnp.sum(e, axis=1, keepdims=True)
    acc = acc * corr + _dot(e.astype(BF16), v)
    return m_new, l, acc


def _flash_out(carry):
    _, l, acc = carry
    return acc / jnp.maximum(l, 1e-30)


def _flash_out_sink(carry, sink_col):
    m, l, acc = carry
    m_f = jnp.maximum(m, sink_col)
    corr = jnp.exp(m - m_f)
    return acc * corr / (l * corr + jnp.exp(sink_col - m_f))


def _stack_heads(q_ref, nq):
    return jnp.concatenate([q_ref[:, hh * HEAD_DIM:(hh + 1) * HEAD_DIM] for hh in range(HPG)], axis=0).astype(BF16)


def _head_cols(ref, base, nq):
    return jnp.concatenate([jnp.full((nq, 1), ref[base + hh], F32) for hh in range(HPG)], axis=0)


def _stacked_pos(q0, nq):
    pos = q0 + lax.broadcasted_iota(jnp.int32, (nq, 1), 0)
    return jnp.concatenate([pos] * HPG, axis=0)


def _tile_rows(x):
    return jnp.concatenate([x] * HPG, axis=0)


def _top_blocks(score, lane):
    sel = jnp.zeros_like(score)
    work = score
    for _ in range(N_SEL):
        mx = jnp.max(work, axis=1, keepdims=True)
        first = jnp.min(jnp.where(work == mx, lane, 1e9), axis=1, keepdims=True)
        hit = lane == first
        sel = jnp.where(hit, 1.0, sel)
        work = jnp.where(hit, -jnp.inf, work)
    return sel


def _cmp_branch(q_all, kc, vc, slope_col, qpos_col, ov_ref, nq, n_blk):
    ncp = kc.shape[0]
    end_row = lax.broadcasted_iota(jnp.int32, (1, ncp), 1) * CMP_STRIDE + (CMP_LEN - 1)
    carry = _flash_tile(_flash_init(q_all.shape[0]), q_all, kc, vc, slope_col, qpos_col, end_row)
    m, l, _ = carry
    o_c = _flash_out(carry)
    dist = qpos_col - end_row
    s = _dot_nt(q_all, kc) * SCALE - slope_col * dist.astype(F32)
    p = jnp.where(dist >= 0, jnp.exp(jnp.where(dist >= 0, s, NEG_INF) - m), 0.0) / jnp.maximum(l, 1e-30)
    imp_c = p[0:nq]
    for hh in range(1, HPG):
        imp_c = imp_c + p[hh * nq:(hh + 1) * nq]
    hi = imp_c.astype(BF16)
    lo = (imp_c - hi.astype(F32)).astype(BF16)
    imp = _dot(hi, ov_ref[...]) + _dot(lo, ov_ref[...])
    nl = imp.shape[1]
    lane_i = lax.broadcasted_iota(jnp.int32, (nq, nl), 1)
    cur = qpos_col[0:nq] // SEL_BLK
    forced = jnp.logical_or(lane_i == 0, jnp.logical_or(lane_i == cur, lane_i == cur - 1))
    score = jnp.where(forced, SEL_FORCE, jnp.where(lane_i <= cur, imp, -SEL_FORCE))
    score = jnp.where(lane_i < n_blk, score, -jnp.inf)
    return o_c, _top_blocks(score, lane_i.astype(F32))


def _gated_sum(gate_ref, bg_ref, branches, o_ref, nq):
    gates = jax.nn.sigmoid(gate_ref[...] + bg_ref[...])
    for hh in range(HPG):
        rows = slice(hh * nq, (hh + 1) * nq)
        o = gates[:, 3 * hh:3 * hh + 1] * branches[0][rows]
        for j in (1, 2):
            o = o + gates[:, 3 * hh + j:3 * hh + j + 1] * branches[j][rows]
        o_ref[:, hh * HEAD_DIM:(hh + 1) * HEAD_DIM] = o


def _chunk_ab(load_rows, wab_ref, c):
    acc = None
    for r in range(CMP_STRIDE):
        t = _dot(load_rows(r).astype(BF16), wab_ref[c, r])
        acc = t if acc is None else acc + t
    return acc


def _cmp_bias(pe_ref, w_ref, c):
    acc = jnp.zeros((8, HEAD_DIM), F32)
    for r in range(CMP_LEN):
        acc = acc + _dot(jnp.broadcast_to(pe_ref[c, r:r + 1, :], (8, HEAD_DIM)).astype(BF16), w_ref[c, r])
    return acc[0:1]


def _blocks_from_ab(ab, bias):
    n = ab.shape[0]
    return ab[:, :HEAD_DIM] + pltpu.roll(ab[:, HEAD_DIM:], n - 1, 0) + bias


def _compress_kernel(nch, kv_ref, wab_ref, w_ref, pe_ref, o_ref):
    for c in range(2):
        bias = _cmp_bias(pe_ref, w_ref, c)
        for g in range(N_KV):
            cg = c * N_KV + g
            ab = _chunk_ab(lambda r: kv_ref[pl.ds(r * KV_ROWS + cg, nch, stride=CMP_STRIDE * KV_ROWS), :], wab_ref, c)
            o_ref[c, g] = _blocks_from_ab(ab, bias)


def _compress_prompt(kv, wab, w, pe, batch, seq):
    nch = seq // CMP_STRIDE
    fixed4 = lambda b: (0, 0, 0, 0)
    return pl.pallas_call(
        functools.partial(_compress_kernel, nch),
        grid=(batch,),
        in_specs=[pl.BlockSpec((seq * KV_ROWS, HEAD_DIM), lambda b: (b, 0)),
                  pl.BlockSpec(wab.shape, fixed4), pl.BlockSpec(w.shape, fixed4),
                  pl.BlockSpec(pe.shape, lambda b: (0, 0, 0))],
        out_specs=pl.BlockSpec((None, 2, N_KV, nch, HEAD_DIM), lambda b: (b, 0, 0, 0, 0)),
        out_shape=jax.ShapeDtypeStruct((batch, 2, N_KV, nch, HEAD_DIM), F32),
        compiler_params=_params(("parallel",), 40),
        name="compress_prompt",
    )(kv, wab, w, pe)


def _nsa_prompt_kernel(tq, seq, slopes_ref, q_ref, gate_ref, bg_ref, kvc_ref, ks_ref, vs_ref, kw_ref, vw_ref,
                       ov_ref, e_ref, o_ref, selx_ref):
    g = pl.program_id(1)
    qi = pl.program_id(2)
    q0 = qi * tq
    q_all = _stack_heads(q_ref, tq)
    slope_col = _head_cols(slopes_ref, g * HPG, tq)
    qpos_col = _stacked_pos(q0, tq)

    o_c, sel = _cmp_branch(q_all, kvc_ref[0].astype(BF16), kvc_ref[1].astype(BF16), slope_col, qpos_col, ov_ref,
                           tq, seq // SEL_BLK)
    selx_ref[...] = _dot(sel.astype(BF16), e_ref[...])

    def key_tile(k_ref, v_ref, kj):
        k0 = pl.multiple_of(kj * tq, tq)
        kpos_row = k0 + lax.broadcasted_iota(jnp.int32, (1, tq), 1)
        return k0, k_ref[pl.ds(k0, tq), :].astype(BF16), v_ref[pl.ds(k0, tq), :].astype(BF16), kpos_row

    def sel_body(kj, carry):
        k0, k, v, kpos_row = key_tile(ks_ref, vs_ref, kj)
        extra = _tile_rows(selx_ref[:, pl.ds(k0, tq)]) > 0.5
        return _flash_tile(carry, q_all, k, v, slope_col, qpos_col, kpos_row, extra=extra)

    def win_body(kj, carry):
        _, k, v, kpos_row = key_tile(kw_ref, vw_ref, kj)
        return _flash_tile(carry, q_all, k, v, slope_col, qpos_col, kpos_row, window=WIN_A)

    rows = HPG * tq
    o_s = _flash_out(lax.fori_loop(0, qi + 1, sel_body, _flash_init(rows)))
    o_w = _flash_out(lax.fori_loop(jnp.maximum(qi - WIN_A // tq, 0), qi + 1, win_body, _flash_init(rows)))
    _gated_sum(gate_ref, bg_ref, (o_c, o_s, o_w), o_ref, tq)


def _nsa_prompt(h, kvc, slopes, bg, ov, expand, batch, seq, tq):
    nq = seq // tq
    nch = kvc.shape[3]
    kv_spec = lambda col: pl.BlockSpec((seq, HEAD_DIM), lambda b, g, i, col=col: (b, col // HEAD_DIM + g))
    return pl.pallas_call(
        functools.partial(_nsa_prompt_kernel, tq, seq),
        grid=(batch, N_KV, nq),
        in_specs=[pl.BlockSpec(memory_space=pltpu.SMEM),
                  pl.BlockSpec((tq, HPG * HEAD_DIM), lambda b, g, i: (b * nq + i, g)),
                  pl.BlockSpec((tq, LANES), lambda b, g, i: (b * nq + i, A_GATE // LANES + g)),
                  pl.BlockSpec((1, LANES), lambda b, g, i: (0, g)),
                  pl.BlockSpec((None, 2, None, nch, HEAD_DIM), lambda b, g, i: (b, 0, g, 0, 0)),
                  kv_spec(A_KS), kv_spec(A_KS + N_KV * HEAD_DIM), kv_spec(A_KW), kv_spec(A_KW + N_KV * HEAD_DIM),
                  pl.BlockSpec(ov.shape, lambda b, g, i: (0, 0)),
                  pl.BlockSpec(expand.shape, lambda b, g, i: (0, 0))],
        out_specs=pl.BlockSpec((tq, HPG * HEAD_DIM), lambda b, g, i: (b * nq + i, g)),
        out_shape=jax.ShapeDtypeStruct((batch * seq, Q_W), F32),
        scratch_shapes=[pltpu.VMEM((tq, seq), F32)],
        compiler_params=_params(("parallel", "parallel", "arbitrary"), 48),
        name="nsa_prompt",
    )(slopes, h, h, bg, kvc, h, h, h, h, ov, expand)


def _swa_prompt_kernel(tq, slopes_ref, sinks_ref, q_ref, k_ref, v_ref, o_ref):
    g = pl.program_id(1)
    qi = pl.program_id(2)
    q_all = _stack_heads(q_ref, tq)
    slope_col = _head_cols(slopes_ref, g * HPG, tq)
    sink_col = _head_cols(sinks_ref, g * HPG, tq)
    qpos_col = _stacked_pos(qi * tq, tq)

    def body(kj, carry):
        k0 = pl.multiple_of(kj * tq, tq)
        kpos_row = k0 + lax.broadcasted_iota(jnp.int32, (1, tq), 1)
        k = k_ref[pl.ds(k0, tq), :].astype(BF16)
        v = v_ref[pl.ds(k0, tq), :].astype(BF16)
        return _flash_tile(carry, q_all, k, v, slope_col, qpos_col, kpos_row, window=WIN_B)

    lo = jnp.maximum(qi - (WIN_B + tq - 1) // tq, 0)
    o = _flash_out_sink(lax.fori_loop(lo, qi + 1, body, _flash_init(HPG * tq)), sink_col)
    for hh in range(HPG):
        o_ref[:, hh * HEAD_DIM:(hh + 1) * HEAD_DIM] = o[hh * tq:(hh + 1) * tq]


def _swa_prompt(h, slopes, sinks, batch, seq, tq):
    nq = seq // tq
    kv_spec = lambda col: pl.BlockSpec((seq, HEAD_DIM), lambda b, g, i, col=col: (b, col // HEAD_DIM + g))
    smem = pl.BlockSpec(memory_space=pltpu.SMEM)
    return pl.pallas_call(
        functools.partial(_swa_prompt_kernel, tq),
        grid=(batch, N_KV, nq),
        in_specs=[smem, smem,
                  pl.BlockSpec((tq, HPG * HEAD_DIM), lambda b, g, i: (b * nq + i, g)),
                  kv_spec(B_KV), kv_spec(B_KV + N_KV * HEAD_DIM)],
        out_specs=pl.BlockSpec((tq, HPG * HEAD_DIM), lambda b, g, i: (b * nq + i, g)),
        out_shape=jax.ShapeDtypeStruct((batch * seq, Q_W), F32),
        compiler_params=_params(("parallel", "parallel", "arbitrary"), 40),
        name="swa_prompt",
    )(slopes, sinks, h, h, h)


def _mem_kernel(row_layout, q_ref, kv_ref, o_ref):
    for hd in range(N_MEM_HEADS):
        cols = slice(hd * HEAD_DIM, (hd + 1) * HEAD_DIM)
        if row_layout:
            n_mem = kv_ref.shape[0] // (2 * N_MEM_HEADS)
            k = kv_ref[pl.ds(hd, n_mem, stride=2 * N_MEM_HEADS), :].astype(BF16)
            v = kv_ref[pl.ds(N_MEM_HEADS + hd, n_mem, stride=2 * N_MEM_HEADS), :].astype(BF16)
        else:
            k = kv_ref[:, hd * HEAD_DIM:(hd + 1) * HEAD_DIM].astype(BF16)
            v = kv_ref[:, QM_W + hd * HEAD_DIM:QM_W + (hd + 1) * HEAD_DIM].astype(BF16)
        s = _dot_nt(q_ref[:, cols].astype(BF16), k) * SCALE
        e = jnp.exp(s - jnp.max(s, axis=1, keepdims=True))
        p = e / jnp.sum(e, axis=1, keepdims=True)
        o_ref[:, cols] = _dot(p.astype(BF16), v)


def _mem_attend(h, qm_col, row0, mkv, layer, tq, blocks_per_batch):
    _, n_batch, kv_rows, kv_cols = mkv.shape
    nblk = n_batch * blocks_per_batch
    return pl.pallas_call(
        functools.partial(_mem_kernel, kv_cols == HEAD_DIM),
        grid=(nblk,),
        in_specs=[pl.BlockSpec((tq, QM_W), lambda i: (row0 // tq + i, qm_col // QM_W)),
                  pl.BlockSpec((None, None, kv_rows, kv_cols), lambda i: (layer, i // blocks_per_batch, 0, 0))],
        out_specs=pl.BlockSpec((tq, QM_W), lambda i: (i, 0)),
        out_shape=jax.ShapeDtypeStruct((nblk * tq, QM_W), F32),
        compiler_params=_params(("parallel",), 32),
        name="mem_attend",
    )(h, mkv)


def _nsa_sample_cmp_kernel(t_new, past, *refs):
    npg = PAGES_PER_STEP
    _, slopes_ref = refs[0], refs[1]
    page_refs = refs[2:2 + npg]
    q_ref, wab_ref, w_ref, pe_ref, ov_ref, oc_ref, sel_ref, ab_ref = refs[2 + npg:]
    j = pl.program_id(1)
    rows_per_step = npg * PAGE_SIZE // CMP_STRIDE
    pages = [jnp.swapaxes(p[...], 0, 1) for p in page_refs]
    for c in range(2):
        for g in range(N_KV):
            cg = c * N_KV + g
            ab = _chunk_ab(lambda r: jnp.concatenate([x[r * KV_ROWS + cg] for x in pages], axis=0), wab_ref, c)
            ab_ref[cg, pl.ds(pl.multiple_of(j * rows_per_step, rows_per_step), rows_per_step), :] = ab

    @pl.when(j == pl.num_programs(1) - 1)
    def _():
        bias = [_cmp_bias(pe_ref, w_ref, c) for c in range(2)]
        n_blk = -(-(past + t_new) // SEL_BLK)
        for g in range(N_KV):
            kc = _blocks_from_ab(ab_ref[g], bias[0]).astype(BF16)
            vc = _blocks_from_ab(ab_ref[N_KV + g], bias[1]).astype(BF16)
            q_all = _stack_heads(q_ref.at[:, g * HPG * HEAD_DIM:(g + 1) * HPG * HEAD_DIM], t_new)
            slope_col = _head_cols(slopes_ref, g * HPG, t_new)
            qpos_col = _stacked_pos(past, t_new)
            o_c, sel = _cmp_branch(q_all, kc, vc, slope_col, qpos_col, ov_ref, t_new, n_blk)
            for hh in range(HPG):
                oc_ref[:, (g * HPG + hh) * HEAD_DIM:(g * HPG + hh + 1) * HEAD_DIM] = o_c[hh * t_new:(hh + 1) * t_new]
            sel_ref[g] = sel


def _nsa_sample_cmp(h, row0, cache, lk, page_table, slopes, wab, w, pe, ov, n_seq, t_new, past):
    n_pages = past // PAGE_SIZE
    npg = PAGES_PER_STEP
    nch = past // CMP_STRIDE
    page_spec = lambda k: pl.BlockSpec((None, None, PAGE_SIZE // CMP_STRIDE, CMP_STRIDE * KV_ROWS, HEAD_DIM),
                                       lambda b, j, pt, k=k: (lk, pt[b, j * npg + k], 0, 0, 0))
    fixed = lambda nd: (lambda b, j, pt: (0,) * nd)
    grid_spec = pltpu.PrefetchScalarGridSpec(
        num_scalar_prefetch=1,
        grid=(n_seq, n_pages // npg),
        in_specs=[pl.BlockSpec(memory_space=pltpu.SMEM)] + [page_spec(k) for k in range(npg)] + [
            pl.BlockSpec((t_new, Q_W), lambda b, j, pt: (row0 // t_new + b, 0)),
            pl.BlockSpec(wab.shape, fixed(4)), pl.BlockSpec(w.shape, fixed(4)), pl.BlockSpec(pe.shape, fixed(3)),
            pl.BlockSpec(ov.shape, fixed(2))],
        out_specs=[pl.BlockSpec((t_new, Q_W), lambda b, j, pt: (b, 0)),
                   pl.BlockSpec((None, N_KV, t_new, ov.shape[1]), lambda b, j, pt: (b, 0, 0, 0))],
        scratch_shapes=[pltpu.VMEM((2 * N_KV, nch, 2 * HEAD_DIM), F32)],
    )
    return pl.pallas_call(
        functools.partial(_nsa_sample_cmp_kernel, t_new, past),
        grid_spec=grid_spec,
        out_shape=[jax.ShapeDtypeStruct((n_seq * t_new, Q_W), F32),
                   jax.ShapeDtypeStruct((n_seq, N_KV, t_new, ov.shape[1]), F32)],
        compiler_params=_params(("parallel", "arbitrary"), 48),
        name="nsa_sample_cmp",
    )(page_table, slopes, *([cache] * npg), h, wab, w, pe, ov)


def _kv_rows(ref, c, g):
    return ref[pl.ds(c * N_KV + g, ref.shape[0] // KV_ROWS, stride=KV_ROWS), :].astype(BF16)


def _pad_keys(new_ref, col, t_new):
    rows = new_ref[:, col:col + HEAD_DIM]
    return jnp.concatenate([rows, jnp.zeros((LANES - t_new, HEAD_DIM), F32)], axis=0).astype(BF16)


def _nsa_sample_sel_kernel(t_new, past, *refs):
    npg = PAGES_PER_STEP
    slopes_ref = refs[1]
    page_refs = refs[2:2 + npg]
    (q_ref, sel_ref, e_ref, oc_ref, ks_new_ref, kw_new_ref, gate_ref, bg_ref, win_ref,
     o_ref, m_ref, l_ref, acc_ref) = refs[2 + npg:]
    j = pl.program_id(1)
    rows = HPG * t_new
    keys = npg * PAGE_SIZE
    qpos_col = _stacked_pos(past, t_new)

    @pl.when(j == 0)
    def _():
        m_ref[...] = jnp.full(m_ref.shape, NEG_INF, F32)
        l_ref[...] = jnp.zeros(l_ref.shape, F32)
        acc_ref[...] = jnp.zeros(acc_ref.shape, F32)

    kpos_row = j * keys + lax.broadcasted_iota(jnp.int32, (1, keys), 1)
    for g in range(N_KV):
        q_all = _stack_heads(q_ref.at[:, g * HPG * HEAD_DIM:(g + 1) * HPG * HEAD_DIM], t_new)
        slope_col = _head_cols(slopes_ref, g * HPG, t_new)
        k = jnp.concatenate([_kv_rows(p, 0, g) for p in page_refs], axis=0)
        v = jnp.concatenate([_kv_rows(p, 1, g) for p in page_refs], axis=0)
        extra = _tile_rows(_dot(sel_ref[g].astype(BF16), e_ref[...])) > 0.5
        carry = _flash_tile((m_ref[g], l_ref[g], acc_ref[g]), q_all, k, v, slope_col, qpos_col, kpos_row, extra=extra)
        m_ref[g], l_ref[g], acc_ref[g] = carry

    @pl.when(j == pl.num_programs(1) - 1)
    def _():
        new_pos = past + lax.broadcasted_iota(jnp.int32, (1, LANES), 1)
        wb = win_ref.shape[0] // KV_ROWS
        win_pos = past - wb + lax.broadcasted_iota(jnp.int32, (1, wb), 1)
        for g in range(N_KV):
            q_all = _stack_heads(q_ref.at[:, g * HPG * HEAD_DIM:(g + 1) * HPG * HEAD_DIM], t_new)
            slope_col = _head_cols(slopes_ref, g * HPG, t_new)
            sel_new = _tile_rows(sel_ref[g][:, past // SEL_BLK:past // SEL_BLK + 1]) > 0.5
            carry = _flash_tile((m_ref[g], l_ref[g], acc_ref[g]), q_all,
                                _pad_keys(ks_new_ref, g * HEAD_DIM, t_new),
                                _pad_keys(ks_new_ref, (N_KV + g) * HEAD_DIM, t_new),
                                slope_col, qpos_col, new_pos, extra=sel_new)
            o_s = _flash_out(carry)
            carry = _flash_tile(_flash_init(rows), q_all, _kv_rows(win_ref, 0, g), _kv_rows(win_ref, 1, g),
                                slope_col, qpos_col, win_pos, window=WIN_A)
            carry = _flash_tile(carry, q_all, _pad_keys(kw_new_ref, g * HEAD_DIM, t_new),
                                _pad_keys(kw_new_ref, (N_KV + g) * HEAD_DIM, t_new),
                                slope_col, qpos_col, new_pos, window=WIN_A)
            o_w = _flash_out(carry)
            o_c = jnp.concatenate([oc_ref[:, (g * HPG + hh) * HEAD_DIM:(g * HPG + hh + 1) * HEAD_DIM]
                                   for hh in range(HPG)], axis=0)
            _gated_sum(gate_ref.at[:, g * LANES:(g + 1) * LANES], bg_ref.at[:, g * LANES:(g + 1) * LANES],
                       (o_c, o_s, o_w), o_ref.at[:, g * HPG * HEAD_DIM:(g + 1) * HPG * HEAD_DIM], t_new)


def _nsa_sample_sel(h, row0, cache, lk, page_table, slopes, sel, expand, oc, bg, win, n_seq, t_new, past):
    n_pages = past // PAGE_SIZE
    npg = PAGES_PER_STEP
    nbl = sel.shape[3]
    win_rows = win.shape[2]
    rows = HPG * t_new
    page_spec = lambda k: pl.BlockSpec((None, None, PAGE_SIZE * KV_ROWS, HEAD_DIM),
                                       lambda b, j, pt, k=k: (lk, pt[b, j * npg + k], 0, 0))
    hrow = lambda width, col: pl.BlockSpec((t_new, width), lambda b, j, pt: (row0 // t_new + b, col // width))
    grid_spec = pltpu.PrefetchScalarGridSpec(
        num_scalar_prefetch=1,
        grid=(n_seq, n_pages // npg),
        in_specs=[pl.BlockSpec(memory_space=pltpu.SMEM)] + [page_spec(k) for k in range(npg)] + [
            hrow(Q_W, 0),
            pl.BlockSpec((None, N_KV, t_new, nbl), lambda b, j, pt: (b, 0, 0, 0)),
            pl.BlockSpec((nbl, npg * PAGE_SIZE), lambda b, j, pt: (0, j)),
            pl.BlockSpec((t_new, Q_W), lambda b, j, pt: (b, 0)),
            hrow(KV_W, A_KS), hrow(KV_W, A_KW), hrow(N_KV * LANES, A_GATE),
            pl.BlockSpec((1, N_KV * LANES), lambda b, j, pt: (0, 0)),
            pl.BlockSpec((None, None, win_rows, HEAD_DIM), lambda b, j, pt: (lk, b, 0, 0))],
        out_specs=pl.BlockSpec((t_new, Q_W), lambda b, j, pt: (b, 0)),
        scratch_shapes=[pltpu.VMEM((N_KV, rows, 1), F32), pltpu.VMEM((N_KV, rows, 1), F32),
                        pltpu.VMEM((N_KV, rows, HEAD_DIM), F32)],
    )
    return pl.pallas_call(
        functools.partial(_nsa_sample_sel_kernel, t_new, past),
        grid_spec=grid_spec,
        out_shape=jax.ShapeDtypeStruct((n_seq * t_new, Q_W), F32),
        compiler_params=_params(("parallel", "arbitrary"), 48),
        name="nsa_sample_sel",
    )(page_table, slopes, *([cache] * npg), h, sel, expand, oc, h, h, h, bg, win)


def _swa_sample_kernel(t_new, past, slopes_ref, sinks_ref, q_ref, new_ref, buf_ref, o_ref):
    wb = buf_ref.shape[0] // KV_ROWS
    qpos_col = _stacked_pos(past, t_new)
    buf_pos = past - wb + lax.broadcasted_iota(jnp.int32, (1, wb), 1)
    new_pos = past + lax.broadcasted_iota(jnp.int32, (1, LANES), 1)
    for g in range(N_KV):
        q_all = _stack_heads(q_ref.at[:, g * HPG * HEAD_DIM:(g + 1) * HPG * HEAD_DIM], t_new)
        slope_col = _head_cols(slopes_ref, g * HPG, t_new)
        sink_col = _head_cols(sinks_ref, g * HPG, t_new)
        carry = _flash_tile(_flash_init(HPG * t_new), q_all, _kv_rows(buf_ref, 0, g), _kv_rows(buf_ref, 1, g),
                            slope_col, qpos_col, buf_pos, window=WIN_B)
        carry = _flash_tile(carry, q_all, _pad_keys(new_ref, g * HEAD_DIM, t_new),
                            _pad_keys(new_ref, (N_KV + g) * HEAD_DIM, t_new),
                            slope_col, qpos_col, new_pos, window=WIN_B)
        o = _flash_out_sink(carry, sink_col)
        for hh in range(HPG):
            o_ref[:, (g * HPG + hh) * HEAD_DIM:(g * HPG + hh + 1) * HEAD_DIM] = o[hh * t_new:(hh + 1) * t_new]


def _swa_sample(h, row0, buf, lk, slopes, sinks, n_seq, t_new, past):
    buf_rows = buf.shape[2]
    smem = pl.BlockSpec(memory_space=pltpu.SMEM)
    return pl.pallas_call(
        functools.partial(_swa_sample_kernel, t_new, past),
        grid=(n_seq,),
        in_specs=[smem, smem,
                  pl.BlockSpec((t_new, Q_W), lambda b: (row0 // t_new + b, 0)),
                  pl.BlockSpec((t_new, KV_W), lambda b: (row0 // t_new + b, B_KV // KV_W)),
                  pl.BlockSpec((None, None, buf_rows, HEAD_DIM), lambda b: (lk, b, 0, 0))],
        out_specs=pl.BlockSpec((t_new, Q_W), lambda b: (b, 0)),
        out_shape=jax.ShapeDtypeStruct((n_seq * t_new, Q_W), F32),
        compiler_params=_params(("parallel",), 32),
        name="swa_sample",
    )(slopes, sinks, h, h, buf)


_STAIR = [PEER_TOPK // (k + 1) for k in range(PEER_TOPK)]
_STAIR_ROWS = -(-sum(_STAIR) // 8) * 8


def _top_values(work, out_ref):
    for k in range(PEER_TOPK):
        mx = jnp.max(work, axis=0, keepdims=True)
        out_ref[k:k + 1, :] = mx
        work = jnp.where(work == mx, -jnp.inf, work)


def _peer_route_kernel(x_ref, wqt_ref, sk_ref, xt_ref, b_ref, c_ref, ea_ref, b0_ref,
                       ta_ref, tb_ref, tc_ref, ts_ref, qt_ref):
    xt = x_ref[...].T.astype(BF16)
    xt_ref[...] = xt
    qt_ref[...] = _dot(wqt_ref[...], xt).astype(BF16)
    tc_ref[...] = jnp.full(tc_ref.shape, -jnp.inf, F32)
    n_col = x_ref.shape[0] // LANES

    for hd in range(PEER_HEADS):
        r0 = 2 * hd * PEER_HALF
        b_ref[hd] = _dot(sk_ref[2 * hd + 1], qt_ref[r0 + PEER_HALF:r0 + 2 * PEER_HALF])
        ea_ref[hd] = _dot(sk_ref[2 * hd], qt_ref[r0:r0 + PEER_HALF])

        def column(col, _, hd=hd):
            cs = pl.ds(pl.multiple_of(col * LANES, LANES), LANES)
            s_a = ea_ref[hd, :, cs]
            _top_values(s_a, ta_ref)
            _top_values(b_ref[hd, :, cs], tb_ref)
            top_b = tb_ref[...]
            off = 0
            for k, n_k in enumerate(_STAIR):
                tc_ref[off:off + n_k, :] = ta_ref[k:k + 1, :] + top_b[0:n_k]
                off += n_k
            _top_values(tc_ref[...], ts_ref)
            sc = ts_ref[...]
            thr = sc[PEER_TOPK - 1:PEER_TOPK]
            z = jnp.sum(jnp.exp(sc - sc[0:1]), axis=0, keepdims=True)
            cut_full = jnp.full(s_a.shape, jnp.inf, F32)
            for k, n_k in enumerate(_STAIR):
                a_k = ta_ref[k:k + 1, :]
                cut = jnp.min(jnp.where(a_k + top_b[0:n_k] >= thr, top_b[0:n_k], jnp.inf), axis=0, keepdims=True)
                cut_full = jnp.where(s_a == a_k, cut, cut_full)
            c_ref[hd, :, cs] = cut_full
            ea_ref[hd, :, cs] = jnp.exp(s_a - ta_ref[0:1, :]) / z
            b0_ref[hd:hd + 1, cs] = tb_ref[0:1, :]
            return 0

        lax.fori_loop(0, n_col, column, 0)


def _peer_route(x, wqt, sk, tb):
    n, d = x.shape
    big = jax.ShapeDtypeStruct((PEER_HEADS, PEER_KEYS, n), F32)
    big_spec = pl.BlockSpec((PEER_HEADS, PEER_KEYS, tb), lambda i: (0, 0, i))
    top = pltpu.VMEM((PEER_TOPK, LANES), F32)
    return pl.pallas_call(
        _peer_route_kernel,
        grid=(n // tb,),
        in_specs=[pl.BlockSpec((tb, d), lambda i: (i, 0)), pl.BlockSpec(wqt.shape, lambda i: (0, 0)),
                  pl.BlockSpec(sk.shape, lambda i: (0, 0, 0))],
        out_specs=[pl.BlockSpec((d, tb), lambda i: (0, i))] + [big_spec] * 3 + [
            pl.BlockSpec((PEER_HEADS, tb), lambda i: (0, i))],
        out_shape=[jax.ShapeDtypeStruct((d, n), BF16)] + [big] * 3 + [jax.ShapeDtypeStruct((PEER_HEADS, n), F32)],
        scratch_shapes=[top, top, pltpu.VMEM((_STAIR_ROWS, LANES), F32), top, pltpu.VMEM((wqt.shape[0], tb), BF16)],
        compiler_params=_params(("parallel",), 48),
        name="peer_route",
    )(x, wqt, sk)


def _peer_expert_kernel(alpha, ec, x_ref, xt_ref, b_ref, c_ref, ea_ref, b0_ref, u_ref, vt_ref, g_ref, beta_ref,
                        o_ref, acc_ref, eb_ref, act_ref, hid_ref):
    j = pl.program_id(1)
    tb = x_ref.shape[0]

    @pl.when(j == 0)
    def _():
        acc_ref[...] = jnp.zeros(acc_ref.shape, F32)
        for hd in range(PEER_HEADS):
            eb_ref[hd] = jnp.exp(b_ref[hd] - b0_ref[hd:hd + 1, :])

    sub = 2 * PEER_KEYS
    for s in range(ec // sub):
        act_ref[s * sub:(s + 1) * sub, :] = _dot(u_ref[s * sub:(s + 1) * sub, :], xt_ref[...])
    tot = None
    for s in range(ec // sub):
        for ii in range(s * sub // PEER_KEYS, (s + 1) * sub // PEER_KEYS):
            i = j * (ec // PEER_KEYS) + ii
            rows = slice(ii * PEER_KEYS, (ii + 1) * PEER_KEYS)
            c_rows = [c_ref[hd, pl.ds(i, 1), :] for hd in range(PEER_HEADS)]
            ea_rows = [ea_ref[hd, pl.ds(i, 1), :] for hd in range(PEER_HEADS)]
            for col in range(tb // LANES):
                cs = slice(col * LANES, (col + 1) * LANES)
                w = None
                for hd in range(PEER_HEADS):
                    t = jnp.where(b_ref[hd, :, cs] >= c_rows[hd][:, cs], eb_ref[hd, :, cs], 0.0) * ea_rows[hd][:, cs]
                    w = t if w is None else w + t
                act = act_ref[rows, cs]
                hid_ref[rows, cs] = (0.5 * act * (1.0 + lax.erf(act * SQRT_HALF)) * w).astype(BF16)
        t = _dot(vt_ref[:, s * sub:(s + 1) * sub], hid_ref[s * sub:(s + 1) * sub, :])
        tot = t if tot is None else tot + t
    acc_ref[...] += tot

    @pl.when(j == pl.num_programs(1) - 1)
    def _():
        o_ref[...] = _layer_norm(alpha * x_ref[...] + acc_ref[...].T, g_ref[...], beta_ref[...])


def _peer_experts(x, route, u, vt, layer, g, beta, alpha, tb, ec):
    n, d = x.shape
    ne = u.shape[1]
    xt, b, c, ea, b0 = route
    big_spec = pl.BlockSpec((PEER_HEADS, PEER_KEYS, tb), lambda i, j: (0, 0, i))
    return pl.pallas_call(
        functools.partial(_peer_expert_kernel, alpha, ec),
        grid=(n // tb, ne // ec),
        in_specs=[pl.BlockSpec((tb, d), lambda i, j: (i, 0)), pl.BlockSpec((d, tb), lambda i, j: (0, i))]
        + [big_spec] * 3 + [
            pl.BlockSpec((PEER_HEADS, tb), lambda i, j: (0, i)),
            pl.BlockSpec((None, ec, d), lambda i, j: (layer, j, 0)),
            pl.BlockSpec((None, d, ec), lambda i, j: (layer, 0, j)),
            pl.BlockSpec((1, d), lambda i, j: (0, 0)), pl.BlockSpec((1, d), lambda i, j: (0, 0))],
        out_specs=pl.BlockSpec((tb, d), lambda i, j: (i, 0)),
        out_shape=jax.ShapeDtypeStruct((n, d), F32),
        scratch_shapes=[pltpu.VMEM((d, tb), F32), pltpu.VMEM((PEER_HEADS, PEER_KEYS, tb), F32),
                        pltpu.VMEM((ec, tb), F32), pltpu.VMEM((ec, tb), BF16)],
        compiler_params=_params(("parallel", "arbitrary"), 56),
        name="peer_experts",
    )(x, xt, b, c, ea, b0, u, vt, g, beta)


def _overlap_table(n_cmp_pad, n_blk, n_lanes):
    c0 = jnp.arange(n_cmp_pad)[:, None] * CMP_STRIDE
    s0 = jnp.arange(n_lanes)[None, :] * SEL_BLK
    ov = (c0 <= s0 + SEL_BLK - 1) & (c0 + CMP_LEN - 1 >= s0) & (jnp.arange(n_lanes)[None, :] < n_blk)
    return ov.astype(BF16)


def _expand_table(n_lanes, n_pos):
    return (jnp.arange(n_lanes)[:, None] == jnp.arange(n_pos)[None, :] // SEL_BLK).astype(BF16)


def _alibi_slopes():
    i = jnp.arange(1, N_HEADS + 1, dtype=F32)
    return jnp.exp2(-8.0 * i / N_HEADS)


def _round_up(x, m):
    return -(-x // m) * m


def kernel(x_prompt, x_sample, mem_prompt, cache_a_cmp_kv, cache_a_sel_kv, cache_a_win_kv, cache_b_kv, cache_mem_kv, page_table, w_in_a, b_gate_a, w_cmp_a, pe_cmp_a, w_in_b, sinks_b, w_mem_kv, w_out, b_out, ln_g, ln_b, peer_wq, peer_subkeys, peer_u, peer_v):
    batch, seq, d = x_prompt.shape
    n_seq, t_new, _ = x_sample.shape
    depth = w_out.shape[0]
    past = page_table.shape[1] * PAGE_SIZE
    n_prompt = batch * seq
    n_tok = n_prompt + n_seq * t_new
    alpha = (2.0 * depth) ** 0.25
    assert past % SEL_BLK == 0 and t_new <= SEL_BLK and n_tok % TB_PEER == 0 and n_tok % TM_PROJ == 0
    tq = min(TQ_ATTN, seq)
    slopes = _alibi_slopes()

    x = jnp.concatenate([x_prompt.reshape(n_prompt, d), x_sample.reshape(n_seq * t_new, d)], axis=0)
    kv_shape = lambda n, t, g: (n, t, 2, g, HEAD_DIM)
    a_cmp_p, a_cmp_s, a_sel_p, a_sel_s, a_win_p, a_win_s, b_kv_p, b_kv_s, mem_p = ([] for _ in range(9))

    n_pool = cache_a_cmp_kv.shape[1]
    cmp_pages = cache_a_cmp_kv.reshape(-1, n_pool, PAGE_SIZE // CMP_STRIDE, CMP_STRIDE * KV_ROWS, HEAD_DIM)
    sel_pages = cache_a_sel_kv.reshape(-1, n_pool, PAGE_SIZE * KV_ROWS, HEAD_DIM)
    win_rows = cache_a_win_kv.reshape(cache_a_win_kv.shape[0], n_seq, -1, HEAD_DIM)
    buf_rows = cache_b_kv.reshape(cache_b_kv.shape[0], n_seq, -1, HEAD_DIM)
    mem_rows = cache_mem_kv.reshape(depth, n_seq, -1, HEAD_DIM)
    peer_ub = peer_u.astype(BF16)
    peer_vt = jnp.swapaxes(peer_v, 1, 2).astype(BF16)

    def shifted_rows(rows, new):
        new = new.reshape(n_seq, t_new * KV_ROWS, HEAD_DIM)
        out = jnp.concatenate([rows[:, t_new * KV_ROWS:], new], axis=1)
        return out.reshape(kv_shape(n_seq, rows.shape[1] // KV_ROWS, N_KV))

    for li in range(depth):
        lk = li // 2
        n_mem = mem_prompt.shape[1]
        mkv_p = _matmul(mem_prompt.reshape(-1, d), w_mem_kv[li].astype(BF16), batch * n_mem, 2 * QM_W)
        mem_p.append(mkv_p.reshape(batch, n_mem, 2, N_MEM_HEADS, HEAD_DIM))
        mkv_p = mkv_p.reshape(1, batch, n_mem, 2 * QM_W)

        if li % 2 == 0:
            w_in = w_in_a[lk]
            g0 = Q_W + 3 * KV_W
            gate_w = w_in[:, g0:g0 + 3 * N_HEADS].reshape(d, N_KV, 3 * HPG)
            gate_w = jnp.pad(gate_w, ((0, 0), (0, 0), (0, LANES - 3 * HPG))).reshape(d, N_KV * LANES)
            w_r = jnp.concatenate([w_in[:, :g0], w_in[:, g0 + 3 * N_HEADS:], gate_w], axis=1).astype(BF16)
            bg = jnp.pad(b_gate_a[lk].reshape(N_KV, 3 * HPG), ((0, 0), (0, LANES - 3 * HPG))).reshape(1, N_KV * LANES)
            h = _matmul(x, w_r, TM_PROJ, A_WIDTH // 6)

            w_cmp = w_cmp_a[lk].astype(BF16)
            wab = jnp.concatenate([w_cmp[:, :CMP_STRIDE], w_cmp[:, CMP_STRIDE:]], axis=-1)
            pe = pe_cmp_a[lk]
            hp = h[:n_prompt].reshape(batch, seq, -1)
            hs = h[n_prompt:].reshape(n_seq, t_new, -1)
            a_cmp_p.append(hp[..., A_KC:A_KC + KV_W].reshape(kv_shape(batch, seq, N_KV)))
            kvc_p = _compress_prompt(a_cmp_p[-1].reshape(-1, HEAD_DIM), wab, w_cmp, pe, batch, seq)
            n_blk_p = seq // SEL_BLK
            ov_p = _overlap_table(seq // CMP_STRIDE, n_blk_p, _round_up(n_blk_p, LANES))
            ex_p = _expand_table(_round_up(n_blk_p, LANES), seq)
            mix_p = _nsa_prompt(h, kvc_p, slopes, bg, ov_p, ex_p, batch, seq, tq)

            n_blk_s = -(-(past + t_new) // SEL_BLK)
            nbl = _round_up(n_blk_s, LANES)
            ov_s = _overlap_table(past // CMP_STRIDE, n_blk_s, nbl)
            ex_s = _expand_table(nbl, past)
            oc_s, sel_s = _nsa_sample_cmp(h, n_prompt, cmp_pages, lk, page_table, slopes, wab, w_cmp, pe, ov_s,
                                          n_seq, t_new, past)
            mix_s = _nsa_sample_sel(h, n_prompt, sel_pages, lk, page_table, slopes, sel_s, ex_s, oc_s, bg, win_rows,
                                    n_seq, t_new, past)

            a_cmp_s.append(hs[..., A_KC:A_KC + KV_W].reshape(kv_shape(n_seq, t_new, N_KV)))
            a_sel_p.append(hp[..., A_KS:A_KS + KV_W].reshape(kv_shape(batch, seq, N_KV)))
            a_sel_s.append(hs[..., A_KS:A_KS + KV_W].reshape(kv_shape(n_seq, t_new, N_KV)))
            wn = min(WIN_A, seq)
            a_win_p.append(hp[:, seq - wn:, A_KW:A_KW + KV_W].reshape(kv_shape(batch, wn, N_KV)))
            a_win_s.append(shifted_rows(win_rows[lk], hs[..., A_KW:A_KW + KV_W]))
            qm_col = A_QM
        else:
            h = _matmul(x, w_in_b[lk].astype(BF16), TM_PROJ, B_WIDTH // 4)
            mix_p = _swa_prompt(h, slopes, sinks_b[lk], batch, seq, tq)
            mix_s = _swa_sample(h, n_prompt, buf_rows, lk, slopes, sinks_b[lk], n_seq, t_new, past)
            hp = h[:n_prompt].reshape(batch, seq, -1)
            hs = h[n_prompt:].reshape(n_seq, t_new, -1)
            wn = min(WIN_B, seq)
            b_kv_p.append(hp[:, seq - wn:, B_KV:B_KV + KV_W].reshape(kv_shape(batch, wn, N_KV)))
            b_kv_s.append(shifted_rows(buf_rows[lk], hs[..., B_KV:B_KV + KV_W]))
            qm_col = B_QM

        mem_o_p = _mem_attend(h, qm_col, 0, mkv_p, 0, tq, seq // tq)
        mem_o_s = _mem_attend(h, qm_col, n_prompt, mem_rows, li, t_new, 1)
        mix = jnp.concatenate([mix_p, mix_s], axis=0)
        mem_o = jnp.concatenate([mem_o_p, mem_o_s], axis=0)
        w_o = w_out[li].astype(BF16)
        x = _outproj_ln(mix, mem_o, w_o[:Q_W], w_o[Q_W:], b_out[li].reshape(1, d), x,
                        ln_g[li, 0].reshape(1, d), ln_b[li, 0].reshape(1, d), alpha, TM_OUT)

        sk = peer_subkeys[li].reshape(2 * PEER_HEADS, PEER_KEYS, PEER_HALF).astype(BF16)
        route = _peer_route(x, peer_wq[li].T.astype(BF16), sk, TB_PEER)
        x = _peer_experts(x, route, peer_ub, peer_vt, li, ln_g[li, 1].reshape(1, d), ln_b[li, 1].reshape(1, d),
                          alpha, TB_PEER, EC_PEER)

    return (x[:n_prompt].reshape(batch, seq, d), x[n_prompt:].reshape(n_seq, t_new, d),
            jnp.stack(a_cmp_p), jnp.stack(a_cmp_s), jnp.stack(a_sel_p), jnp.stack(a_sel_s),
            jnp.stack(a_win_p), jnp.stack(a_win_s), jnp.stack(b_kv_p), jnp.stack(b_kv_s), jnp.stack(mem_p))
```

```python
import functools
import math

import jax
import jax.numpy as jnp
from jax import lax
from jax.experimental import pallas as pl
from jax.experimental.pallas import tpu as pltpu

F32 = jnp.float32
BF16 = jnp.bfloat16

HEAD_DIM = 128
N_MEM_HEADS = 4
N_KV = 2
HPG = 6
N_HEADS = N_KV * HPG
Q_W = N_HEADS * HEAD_DIM
KV_W = 2 * N_KV * HEAD_DIM
KV_ROWS = 2 * N_KV
QM_W = N_MEM_HEADS * HEAD_DIM
CMP_STRIDE = 16
CMP_LEN = 32
SEL_BLK = 64
N_SEL = 16
WIN_A = 512
WIN_B = 128
PAGE_SIZE = 128
PEER_HEADS = 8
PEER_KEYS = 128
PEER_TOPK = 16
PEER_HALF = 128
LN_EPS = 1e-5
NEG_INF = -1e30
SEL_FORCE = 1e4
SCALE = HEAD_DIM ** -0.5
SQRT_HALF = math.sqrt(0.5)
LANES = 128
MXU_N = 256
ACC_ROWS = 512
GATE_ROWS = 64

TM_PROJ = 1024
TM_OUT = 512
TQ_ATTN = 256
TB_PEER = 512
EC_PEER = 512
PAGES_PER_STEP = 32
MEM_SEQS_PER_STEP = 8

A_KC, A_KS, A_KW, A_QM, A_GATE = Q_W, Q_W + KV_W, Q_W + 2 * KV_W, Q_W + 3 * KV_W, Q_W + 3 * KV_W + QM_W
A_WIDTH = A_GATE + N_KV * LANES
B_KV, B_QM = Q_W, Q_W + KV_W
B_WIDTH = B_QM + QM_W


def _params(sem, vmem_mb, flags=None):
    return pltpu.CompilerParams(dimension_semantics=sem, vmem_limit_bytes=vmem_mb << 20, flags=flags)


def _dot(a, b):
    return jnp.dot(a, b, preferred_element_type=F32)


def _dot_nt(a, b):
    return lax.dot_general(a, b, (((1,), (1,)), ((), ())), preferred_element_type=F32)


def _mm_kernel(x_ref, w_ref, o_ref):
    o_ref[...] = _dot(x_ref[...].astype(BF16), w_ref[...])


def _matmul(x, w, tm, tn):
    m, k = x.shape
    n = w.shape[1]
    return pl.pallas_call(
        _mm_kernel,
        grid=(m // tm, n // tn),
        in_specs=[pl.BlockSpec((tm, k), lambda i, j: (i, 0)), pl.BlockSpec((k, tn), lambda i, j: (0, j))],
        out_specs=pl.BlockSpec((tm, tn), lambda i, j: (i, j)),
        out_shape=jax.ShapeDtypeStruct((m, n), F32),
        compiler_params=_params(("parallel", "parallel"), 40),
        name="proj",
    )(x, w)


def _layer_norm(z, g, b):
    mu = jnp.mean(z, axis=-1, keepdims=True)
    zc = z - mu
    var = jnp.mean(zc * zc, axis=-1, keepdims=True)
    return zc * lax.rsqrt(var + LN_EPS) * g + b


def _outproj_kernel(alpha, mix_ref, mem_ref, w1_ref, w2_ref, bias_ref, x_ref, g_ref, b_ref, o_ref):
    y = _dot(mix_ref[...].astype(BF16), w1_ref[...]) + _dot(mem_ref[...].astype(BF16), w2_ref[...]) + bias_ref[...]
    o_ref[...] = _layer_norm(alpha * x_ref[...] + y, g_ref[...], b_ref[...])


def _outproj_ln(mix, mem, w1, w2, bias, x, g, b, alpha, tm):
    m, d = x.shape
    row = lambda i: (i, 0)
    fixed = lambda i: (0, 0)
    return pl.pallas_call(
        functools.partial(_outproj_kernel, alpha),
        grid=(m // tm,),
        in_specs=[pl.BlockSpec((tm, Q_W), row), pl.BlockSpec((tm, QM_W), row),
                  pl.BlockSpec((Q_W, d), fixed), pl.BlockSpec((QM_W, d), fixed), pl.BlockSpec((1, d), fixed),
                  pl.BlockSpec((tm, d), row), pl.BlockSpec((1, d), fixed), pl.BlockSpec((1, d), fixed)],
        out_specs=pl.BlockSpec((tm, d), row),
        out_shape=jax.ShapeDtypeStruct((m, d), F32),
        compiler_params=_params(("parallel",), 48),
        name="outproj_ln",
    )(mix, mem, w1, w2, bias, x, g, b)


def _flash_init(rows):
    return (jnp.full((rows, 1), NEG_INF, F32), jnp.zeros((rows, 1), F32), jnp.zeros((rows, HEAD_DIM), F32))


def _flash_tile(carry, q, k, v, slope_col, qpos_col, kpos_row, window=None, extra=None):
    m, l, acc = carry
    dist = qpos_col - kpos_row
    s = _dot_nt(q, k) * SCALE - slope_col * dist.astype(F32)
    mask = dist >= 0
    if window is not None:
        mask = jnp.logical_and(mask, dist < window)
    if extra is not None:
        mask = jnp.logical_and(mask, extra)
    s = jnp.where(mask, s, NEG_INF)
    m_new = jnp.maximum(m, jnp.max(s, axis=1, keepdims=True))
    e = jnp.where(mask, jnp.exp(s - m_new), 0.0)
    corr = jnp.exp(m - m_new)
    l = l * corr + jnp.sum(e, axis=1, keepdims=True)
    acc = acc * corr + _dot(e.astype(BF16), v)
    return m_new, l, acc


def _flash_out(carry):
    _, l, acc = carry
    return acc / jnp.maximum(l, 1e-30)


def _flash_out_sink(carry, sink_col):
    m, l, acc = carry
    m_f = jnp.maximum(m, sink_col)
    corr = jnp.exp(m - m_f)
    return acc * corr / (l * corr + jnp.exp(sink_col - m_f))


def _stack_heads(q_ref, nq):
    return jnp.concatenate([q_ref[:, hh * HEAD_DIM:(hh + 1) * HEAD_DIM] for hh in range(HPG)], axis=0).astype(BF16)


def _head_cols(ref, base, nq):
    return jnp.concatenate([jnp.full((nq, 1), ref[base + hh], F32) for hh in range(HPG)], axis=0)


def _stacked_pos(q0, nq):
    pos = q0 + lax.broadcasted_iota(jnp.int32, (nq, 1), 0)
    return jnp.concatenate([pos] * HPG, axis=0)


def _tile_rows(x):
    return jnp.concatenate([x] * HPG, axis=0)


def _top_blocks(score, lane):
    sel = jnp.zeros_like(score)
    work = score
    for _ in range(N_SEL):
        mx = jnp.max(work, axis=1, keepdims=True)
        first = jnp.min(jnp.where(work == mx, lane, 1e9), axis=1, keepdims=True)
        hit = lane == first
        sel = jnp.where(hit, 1.0, sel)
        work = jnp.where(hit, -jnp.inf, work)
    return sel


def _cmp_branch(q_all, kc, vc, slope_col, qpos_col, ov_ref, nq, n_blk):
    ncp = kc.shape[0]
    end_row = lax.broadcasted_iota(jnp.int32, (1, ncp), 1) * CMP_STRIDE + (CMP_LEN - 1)
    dist = qpos_col - end_row
    mask = dist >= 0
    s = jnp.where(mask, _dot_nt(q_all, kc) * SCALE - slope_col * dist.astype(F32), NEG_INF)
    e = jnp.where(mask, jnp.exp(s - jnp.max(s, axis=1, keepdims=True)), 0.0)
    p = e / jnp.maximum(jnp.sum(e, axis=1, keepdims=True), 1e-30)
    o_c = _dot(p.astype(BF16), vc)
    imp_c = p[0:nq]
    for hh in range(1, HPG):
        imp_c = imp_c + p[hh * nq:(hh + 1) * nq]
    hi = imp_c.astype(BF16)
    lo = (imp_c - hi.astype(F32)).astype(BF16)
    imp = _dot(hi, ov_ref[...]) + _dot(lo, ov_ref[...])
    nl = imp.shape[1]
    lane_i = lax.broadcasted_iota(jnp.int32, (nq, nl), 1)
    cur = qpos_col[0:nq] // SEL_BLK
    forced = jnp.logical_or(lane_i == 0, jnp.logical_or(lane_i == cur, lane_i == cur - 1))
    score = jnp.where(forced, SEL_FORCE, jnp.where(lane_i <= cur, imp, -SEL_FORCE))
    score = jnp.where(lane_i < n_blk, score, -jnp.inf)
    return o_c, _top_blocks(score, lane_i.astype(F32))


def _gated_sum(gate_ref, bg_ref, branches, o_ref, nq):
    gates = jax.nn.sigmoid(gate_ref[...] + bg_ref[...])
    for hh in range(HPG):
        rows = slice(hh * nq, (hh + 1) * nq)
        o = gates[:, 3 * hh:3 * hh + 1] * branches[0][rows]
        for j in (1, 2):
            o = o + gates[:, 3 * hh + j:3 * hh + j + 1] * branches[j][rows]
        o_ref[:, hh * HEAD_DIM:(hh + 1) * HEAD_DIM] = o


def _chunk_ab(load_rows, wab_ref, c):
    acc = None
    for r in range(CMP_STRIDE):
        t = _dot(load_rows(r).astype(BF16), wab_ref[c, r])
        acc = t if acc is None else acc + t
    return acc


def _cmp_bias(pe_ref, w_ref, c):
    acc = jnp.zeros((8, HEAD_DIM), F32)
    for r in range(CMP_LEN):
        acc = acc + _dot(jnp.broadcast_to(pe_ref[c, r:r + 1, :], (8, HEAD_DIM)).astype(BF16), w_ref[c, r])
    return acc[0:1]


def _blocks_from_ab(ab, bias):
    n = ab.shape[0]
    return ab[:, :HEAD_DIM] + pltpu.roll(ab[:, HEAD_DIM:], n - 1, 0) + bias


def _compress_kernel(nch, kv_ref, wab_ref, w_ref, pe_ref, o_ref):
    for c in range(2):
        bias = _cmp_bias(pe_ref, w_ref, c)
        for g in range(N_KV):
            cg = c * N_KV + g
            ab = _chunk_ab(lambda r: kv_ref[pl.ds(r * KV_ROWS + cg, nch, stride=CMP_STRIDE * KV_ROWS), :], wab_ref, c)
            o_ref[c, g] = _blocks_from_ab(ab, bias)


def _compress_prompt(kv, wab, w, pe, batch, seq):
    nch = seq // CMP_STRIDE
    fixed4 = lambda b: (0, 0, 0, 0)
    return pl.pallas_call(
        functools.partial(_compress_kernel, nch),
        grid=(batch,),
        in_specs=[pl.BlockSpec((seq * KV_ROWS, HEAD_DIM), lambda b: (b, 0)),
                  pl.BlockSpec(wab.shape, fixed4), pl.BlockSpec(w.shape, fixed4),
                  pl.BlockSpec(pe.shape, lambda b: (0, 0, 0))],
        out_specs=pl.BlockSpec((None, 2, N_KV, nch, HEAD_DIM), lambda b: (b, 0, 0, 0, 0)),
        out_shape=jax.ShapeDtypeStruct((batch, 2, N_KV, nch, HEAD_DIM), F32),
        compiler_params=_params(("parallel",), 40),
        name="compress_prompt",
    )(kv, wab, w, pe)


def _nsa_prompt_kernel(tq, seq, slopes_ref, q_ref, gate_ref, bg_ref, kvc_ref, ks_ref, vs_ref, kw_ref, vw_ref,
                       ov_ref, e_ref, o_ref, selx_ref):
    g = pl.program_id(1)
    qi = pl.program_id(2)
    q0 = qi * tq
    q_all = _stack_heads(q_ref, tq)
    slope_col = _head_cols(slopes_ref, g * HPG, tq)
    qpos_col = _stacked_pos(q0, tq)

    o_c, sel = _cmp_branch(q_all, kvc_ref[0].astype(BF16), kvc_ref[1].astype(BF16), slope_col, qpos_col, ov_ref,
                           tq, seq // SEL_BLK)
    selx_ref[...] = _dot(sel.astype(BF16), e_ref[...])

    def key_tile(k_ref, v_ref, kj):
        k0 = pl.multiple_of(kj * tq, tq)
        kpos_row = k0 + lax.broadcasted_iota(jnp.int32, (1, tq), 1)
        return k0, k_ref[pl.ds(k0, tq), :].astype(BF16), v_ref[pl.ds(k0, tq), :].astype(BF16), kpos_row

    def sel_body(kj, carry):
        k0, k, v, kpos_row = key_tile(ks_ref, vs_ref, kj)
        extra = _tile_rows(selx_ref[:, pl.ds(k0, tq)]) > 0.5
        return _flash_tile(carry, q_all, k, v, slope_col, qpos_col, kpos_row, extra=extra)

    def win_body(kj, carry):
        _, k, v, kpos_row = key_tile(kw_ref, vw_ref, kj)
        return _flash_tile(carry, q_all, k, v, slope_col, qpos_col, kpos_row, window=WIN_A)

    rows = HPG * tq
    o_s = _flash_out(lax.fori_loop(0, qi + 1, sel_body, _flash_init(rows)))
    o_w = _flash_out(lax.fori_loop(jnp.maximum(qi - WIN_A // tq, 0), qi + 1, win_body, _flash_init(rows)))
    _gated_sum(gate_ref, bg_ref, (o_c, o_s, o_w), o_ref, tq)


def _nsa_prompt(h, kvc, slopes, bg, ov, expand, batch, seq, tq):
    nq = seq // tq
    nch = kvc.shape[3]
    kv_spec = lambda col: pl.BlockSpec((seq, HEAD_DIM), lambda b, g, i, col=col: (b, col // HEAD_DIM + g))
    return pl.pallas_call(
        functools.partial(_nsa_prompt_kernel, tq, seq),
        grid=(batch, N_KV, nq),
        in_specs=[pl.BlockSpec(memory_space=pltpu.SMEM),
                  pl.BlockSpec((tq, HPG * HEAD_DIM), lambda b, g, i: (b * nq + i, g)),
                  pl.BlockSpec((tq, LANES), lambda b, g, i: (b * nq + i, A_GATE // LANES + g)),
                  pl.BlockSpec((1, LANES), lambda b, g, i: (0, g)),
                  pl.BlockSpec((None, 2, None, nch, HEAD_DIM), lambda b, g, i: (b, 0, g, 0, 0)),
                  kv_spec(A_KS), kv_spec(A_KS + N_KV * HEAD_DIM), kv_spec(A_KW), kv_spec(A_KW + N_KV * HEAD_DIM),
                  pl.BlockSpec(ov.shape, lambda b, g, i: (0, 0)),
                  pl.BlockSpec(expand.shape, lambda b, g, i: (0, 0))],
        out_specs=pl.BlockSpec((tq, HPG * HEAD_DIM), lambda b, g, i: (b * nq + i, g)),
        out_shape=jax.ShapeDtypeStruct((batch * seq, Q_W), F32),
        scratch_shapes=[pltpu.VMEM((tq, seq), F32)],
        compiler_params=_params(("parallel", "parallel", "arbitrary"), 48),
        name="nsa_prompt",
    )(slopes, h, h, bg, kvc, h, h, h, h, ov, expand)


def _swa_prompt_kernel(tq, slopes_ref, sinks_ref, q_ref, k_ref, v_ref, o_ref):
    g = pl.program_id(1)
    qi = pl.program_id(2)
    q_all = _stack_heads(q_ref, tq)
    slope_col = _head_cols(slopes_ref, g * HPG, tq)
    sink_col = _head_cols(sinks_ref, g * HPG, tq)
    qpos_col = _stacked_pos(qi * tq, tq)

    def body(kj, carry):
        k0 = pl.multiple_of(kj * tq, tq)
        kpos_row = k0 + lax.broadcasted_iota(jnp.int32, (1, tq), 1)
        k = k_ref[pl.ds(k0, tq), :].astype(BF16)
        v = v_ref[pl.ds(k0, tq), :].astype(BF16)
        return _flash_tile(carry, q_all, k, v, slope_col, qpos_col, kpos_row, window=WIN_B)

    lo = jnp.maximum(qi - (WIN_B + tq - 1) // tq, 0)
    o = _flash_out_sink(lax.fori_loop(lo, qi + 1, body, _flash_init(HPG * tq)), sink_col)
    for hh in range(HPG):
        o_ref[:, hh * HEAD_DIM:(hh + 1) * HEAD_DIM] = o[hh * tq:(hh + 1) * tq]


def _swa_prompt(h, slopes, sinks, batch, seq, tq):
    nq = seq // tq
    kv_spec = lambda col: pl.BlockSpec((seq, HEAD_DIM), lambda b, g, i, col=col: (b, col // HEAD_DIM + g))
    smem = pl.BlockSpec(memory_space=pltpu.SMEM)
    return pl.pallas_call(
        functools.partial(_swa_prompt_kernel, tq),
        grid=(batch, N_KV, nq),
        in_specs=[smem, smem,
                  pl.BlockSpec((tq, HPG * HEAD_DIM), lambda b, g, i: (b * nq + i, g)),
                  kv_spec(B_KV), kv_spec(B_KV + N_KV * HEAD_DIM)],
        out_specs=pl.BlockSpec((tq, HPG * HEAD_DIM), lambda b, g, i: (b * nq + i, g)),
        out_shape=jax.ShapeDtypeStruct((batch * seq, Q_W), F32),
        compiler_params=_params(("parallel", "parallel", "arbitrary"), 40),
        name="swa_prompt",
    )(slopes, sinks, h, h, h)


def _mem_kernel(row_layout, q_ref, kv_ref, o_ref):
    n_sub = kv_ref.shape[0]
    tq = q_ref.shape[0] // n_sub
    for sub in range(n_sub):
        rows = slice(sub * tq, (sub + 1) * tq)
        for hd in range(N_MEM_HEADS):
            cols = slice(hd * HEAD_DIM, (hd + 1) * HEAD_DIM)
            if row_layout:
                n_mem = kv_ref.shape[1] // (2 * N_MEM_HEADS)
                k = kv_ref[sub, pl.ds(hd, n_mem, stride=2 * N_MEM_HEADS), :].astype(BF16)
                v = kv_ref[sub, pl.ds(N_MEM_HEADS + hd, n_mem, stride=2 * N_MEM_HEADS), :].astype(BF16)
            else:
                k = kv_ref[sub, :, hd * HEAD_DIM:(hd + 1) * HEAD_DIM].astype(BF16)
                v = kv_ref[sub, :, QM_W + hd * HEAD_DIM:QM_W + (hd + 1) * HEAD_DIM].astype(BF16)
            s = _dot_nt(q_ref[rows, cols].astype(BF16), k) * SCALE
            e = jnp.exp(s - jnp.max(s, axis=1, keepdims=True))
            p = e / jnp.sum(e, axis=1, keepdims=True)
            o_ref[rows, cols] = _dot(p.astype(BF16), v)


def _mem_attend(h, qm_col, row0, mkv, layer, tq, blocks_per_batch, n_sub=1):
    _, n_batch, kv_rows, kv_cols = mkv.shape
    assert n_sub == 1 or blocks_per_batch == 1
    nblk = n_batch * blocks_per_batch // n_sub
    rows = tq * n_sub
    return pl.pallas_call(
        functools.partial(_mem_kernel, kv_cols == HEAD_DIM),
        grid=(nblk,),
        in_specs=[pl.BlockSpec((rows, QM_W), lambda i: (row0 // rows + i, qm_col // QM_W)),
                  pl.BlockSpec((None, n_sub, kv_rows, kv_cols), lambda i: (layer, i // blocks_per_batch, 0, 0))],
        out_specs=pl.BlockSpec((rows, QM_W), lambda i: (i, 0)),
        out_shape=jax.ShapeDtypeStruct((nblk * rows, QM_W), F32),
        compiler_params=_params(("parallel",), 32),
        name="mem_attend",
    )(h, mkv)


def _nsa_sample_cmp_kernel(t_new, past, *refs):
    npg = PAGES_PER_STEP
    _, slopes_ref = refs[0], refs[1]
    page_refs = refs[2:2 + npg]
    q_ref, wab_ref, w_ref, pe_ref, ov_ref, oc_ref, sel_ref, ab_ref = refs[2 + npg:]
    j = pl.program_id(1)
    rows_per_step = npg * PAGE_SIZE // CMP_STRIDE
    pages = [jnp.swapaxes(p[...], 0, 1) for p in page_refs]
    for c in range(2):
        for g in range(N_KV):
            cg = c * N_KV + g
            ab = _chunk_ab(lambda r: jnp.concatenate([x[r * KV_ROWS + cg] for x in pages], axis=0), wab_ref, c)
            ab_ref[cg, pl.ds(pl.multiple_of(j * rows_per_step, rows_per_step), rows_per_step), :] = ab

    @pl.when(j == pl.num_programs(1) - 1)
    def _():
        bias = [_cmp_bias(pe_ref, w_ref, c) for c in range(2)]
        n_blk = -(-(past + t_new) // SEL_BLK)
        for g in range(N_KV):
            kc = _blocks_from_ab(ab_ref[g], bias[0]).astype(BF16)
            vc = _blocks_from_ab(ab_ref[N_KV + g], bias[1]).astype(BF16)
            q_all = _stack_heads(q_ref.at[:, g * HPG * HEAD_DIM:(g + 1) * HPG * HEAD_DIM], t_new)
            slope_col = _head_cols(slopes_ref, g * HPG, t_new)
            qpos_col = _stacked_pos(past, t_new)
            o_c, sel = _cmp_branch(q_all, kc, vc, slope_col, qpos_col, ov_ref, t_new, n_blk)
            for hh in range(HPG):
                oc_ref[:, (g * HPG + hh) * HEAD_DIM:(g * HPG + hh + 1) * HEAD_DIM] = o_c[hh * t_new:(hh + 1) * t_new]
            sel_ref[g] = sel


def _nsa_sample_cmp(h, row0, cache, lk, page_table, slopes, wab, w, pe, ov, n_seq, t_new, past):
    n_pages = past // PAGE_SIZE
    npg = PAGES_PER_STEP
    nch = past // CMP_STRIDE
    page_spec = lambda k: pl.BlockSpec((None, None, PAGE_SIZE // CMP_STRIDE, CMP_STRIDE * KV_ROWS, HEAD_DIM),
                                       lambda b, j, pt, k=k: (lk, pt[b, j * npg + k], 0, 0, 0))
    fixed = lambda nd: (lambda b, j, pt: (0,) * nd)
    grid_spec = pltpu.PrefetchScalarGridSpec(
        num_scalar_prefetch=1,
        grid=(n_seq, n_pages // npg),
        in_specs=[pl.BlockSpec(memory_space=pltpu.SMEM)] + [page_spec(k) for k in range(npg)] + [
            pl.BlockSpec((t_new, Q_W), lambda b, j, pt: (row0 // t_new + b, 0)),
            pl.BlockSpec(wab.shape, fixed(4)), pl.BlockSpec(w.shape, fixed(4)), pl.BlockSpec(pe.shape, fixed(3)),
            pl.BlockSpec(ov.shape, fixed(2))],
        out_specs=[pl.BlockSpec((t_new, Q_W), lambda b, j, pt: (b, 0)),
                   pl.BlockSpec((None, N_KV, t_new, ov.shape[1]), lambda b, j, pt: (b, 0, 0, 0))],
        scratch_shapes=[pltpu.VMEM((2 * N_KV, nch, 2 * HEAD_DIM), F32)],
    )
    return pl.pallas_call(
        functools.partial(_nsa_sample_cmp_kernel, t_new, past),
        grid_spec=grid_spec,
        out_shape=[jax.ShapeDtypeStruct((n_seq * t_new, Q_W), F32),
                   jax.ShapeDtypeStruct((n_seq, N_KV, t_new, ov.shape[1]), F32)],
        compiler_params=_params(("parallel", "arbitrary"), 48),
        name="nsa_sample_cmp",
    )(page_table, slopes, *([cache] * npg), h, wab, w, pe, ov)


def _kv_rows(ref, c, g):
    return ref[pl.ds(c * N_KV + g, ref.shape[0] // KV_ROWS, stride=KV_ROWS), :].astype(BF16)


def _pad_keys(new_ref, col, t_new):
    rows = new_ref[:, col:col + HEAD_DIM]
    return jnp.concatenate([rows, jnp.zeros((LANES - t_new, HEAD_DIM), F32)], axis=0).astype(BF16)


def _nsa_sample_sel_kernel(t_new, past, *refs):
    npg = PAGES_PER_STEP
    slopes_ref = refs[1]
    page_refs = refs[2:2 + npg]
    (q_ref, sel_ref, e_ref, oc_ref, ks_new_ref, kw_new_ref, gate_ref, bg_ref, win_ref,
     o_ref, m_ref, l_ref, acc_ref) = refs[2 + npg:]
    j = pl.program_id(1)
    rows = HPG * t_new
    keys = npg * PAGE_SIZE
    qpos_col = _stacked_pos(past, t_new)

    @pl.when(j == 0)
    def _():
        m_ref[...] = jnp.full(m_ref.shape, NEG_INF, F32)
        l_ref[...] = jnp.zeros(l_ref.shape, F32)
        acc_ref[...] = jnp.zeros(acc_ref.shape, F32)

    kpos_row = j * keys + lax.broadcasted_iota(jnp.int32, (1, keys), 1)
    for g in range(N_KV):
        q_all = _stack_heads(q_ref.at[:, g * HPG * HEAD_DIM:(g + 1) * HPG * HEAD_DIM], t_new)
        slope_col = _head_cols(slopes_ref, g * HPG, t_new)
        k = jnp.concatenate([_kv_rows(p, 0, g) for p in page_refs], axis=0)
        v = jnp.concatenate([_kv_rows(p, 1, g) for p in page_refs], axis=0)
        extra = _tile_rows(_dot(sel_ref[g].astype(BF16), e_ref[...])) > 0.5
        carry = _flash_tile((m_ref[g], l_ref[g], acc_ref[g]), q_all, k, v, slope_col, qpos_col, kpos_row, extra=extra)
        m_ref[g], l_ref[g], acc_ref[g] = carry

    @pl.when(j == pl.num_programs(1) - 1)
    def _():
        new_pos = past + lax.broadcasted_iota(jnp.int32, (1, LANES), 1)
        wb = win_ref.shape[0] // KV_ROWS
        win_pos = past - wb + lax.broadcasted_iota(jnp.int32, (1, wb), 1)
        for g in range(N_KV):
            q_all = _stack_heads(q_ref.at[:, g * HPG * HEAD_DIM:(g + 1) * HPG * HEAD_DIM], t_new)
            slope_col = _head_cols(slopes_ref, g * HPG, t_new)
            sel_new = _tile_rows(sel_ref[g][:, past // SEL_BLK:past // SEL_BLK + 1]) > 0.5
            carry = _flash_tile((m_ref[g], l_ref[g], acc_ref[g]), q_all,
                                _pad_keys(ks_new_ref, g * HEAD_DIM, t_new),
                                _pad_keys(ks_new_ref, (N_KV + g) * HEAD_DIM, t_new),
                                slope_col, qpos_col, new_pos, extra=sel_new)
            o_s = _flash_out(carry)
            carry = _flash_tile(_flash_init(rows), q_all, _kv_rows(win_ref, 0, g), _kv_rows(win_ref, 1, g),
                                slope_col, qpos_col, win_pos, window=WIN_A)
            carry = _flash_tile(carry, q_all, _pad_keys(kw_new_ref, g * HEAD_DIM, t_new),
                                _pad_keys(kw_new_ref, (N_KV + g) * HEAD_DIM, t_new),
                                slope_col, qpos_col, new_pos, window=WIN_A)
            o_w = _flash_out(carry)
            o_c = jnp.concatenate([oc_ref[:, (g * HPG + hh) * HEAD_DIM:(g * HPG + hh + 1) * HEAD_DIM]
                                   for hh in range(HPG)], axis=0)
            _gated_sum(gate_ref.at[:, g * LANES:(g + 1) * LANES], bg_ref.at[:, g * LANES:(g + 1) * LANES],
                       (o_c, o_s, o_w), o_ref.at[:, g * HPG * HEAD_DIM:(g + 1) * HPG * HEAD_DIM], t_new)


def _nsa_sample_sel(h, row0, cache, lk, page_table, slopes, sel, expand, oc, bg, win, n_seq, t_new, past):
    n_pages = past // PAGE_SIZE
    npg = PAGES_PER_STEP
    nbl = sel.shape[3]
    win_rows = win.shape[2]
    rows = HPG * t_new
    page_spec = lambda k: pl.BlockSpec((None, None, PAGE_SIZE * KV_ROWS, HEAD_DIM),
                                       lambda b, j, pt, k=k: (lk, pt[b, j * npg + k], 0, 0))
    hrow = lambda width, col: pl.BlockSpec((t_new, width), lambda b, j, pt: (row0 // t_new + b, col // width))
    grid_spec = pltpu.PrefetchScalarGridSpec(
        num_scalar_prefetch=1,
        grid=(n_seq, n_pages // npg),
        in_specs=[pl.BlockSpec(memory_space=pltpu.SMEM)] + [page_spec(k) for k in range(npg)] + [
            hrow(Q_W, 0),
            pl.BlockSpec((None, N_KV, t_new, nbl), lambda b, j, pt: (b, 0, 0, 0)),
            pl.BlockSpec((nbl, npg * PAGE_SIZE), lambda b, j, pt: (0, j)),
            pl.BlockSpec((t_new, Q_W), lambda b, j, pt: (b, 0)),
            hrow(KV_W, A_KS), hrow(KV_W, A_KW), hrow(N_KV * LANES, A_GATE),
            pl.BlockSpec((1, N_KV * LANES), lambda b, j, pt: (0, 0)),
            pl.BlockSpec((None, None, win_rows, HEAD_DIM), lambda b, j, pt: (lk, b, 0, 0))],
        out_specs=pl.BlockSpec((t_new, Q_W), lambda b, j, pt: (b, 0)),
        scratch_shapes=[pltpu.VMEM((N_KV, rows, 1), F32), pltpu.VMEM((N_KV, rows, 1), F32),
                        pltpu.VMEM((N_KV, rows, HEAD_DIM), F32)],
    )
    return pl.pallas_call(
        functools.partial(_nsa_sample_sel_kernel, t_new, past),
        grid_spec=grid_spec,
        out_shape=jax.ShapeDtypeStruct((n_seq * t_new, Q_W), F32),
        compiler_params=_params(("parallel", "arbitrary"), 48),
        name="nsa_sample_sel",
    )(page_table, slopes, *([cache] * npg), h, sel, expand, oc, h, h, h, bg, win)


def _swa_sample_kernel(t_new, past, slopes_ref, sinks_ref, q_ref, new_ref, buf_ref, o_ref):
    wb = buf_ref.shape[0] // KV_ROWS
    qpos_col = _stacked_pos(past, t_new)
    buf_pos = past - wb + lax.broadcasted_iota(jnp.int32, (1, wb), 1)
    new_pos = past + lax.broadcasted_iota(jnp.int32, (1, LANES), 1)
    for g in range(N_KV):
        q_all = _stack_heads(q_ref.at[:, g * HPG * HEAD_DIM:(g + 1) * HPG * HEAD_DIM], t_new)
        slope_col = _head_cols(slopes_ref, g * HPG, t_new)
        sink_col = _head_cols(sinks_ref, g * HPG, t_new)
        carry = _flash_tile(_flash_init(HPG * t_new), q_all, _kv_rows(buf_ref, 0, g), _kv_rows(buf_ref, 1, g),
                            slope_col, qpos_col, buf_pos, window=WIN_B)
        carry = _flash_tile(carry, q_all, _pad_keys(new_ref, g * HEAD_DIM, t_new),
                            _pad_keys(new_ref, (N_KV + g) * HEAD_DIM, t_new),
                            slope_col, qpos_col, new_pos, window=WIN_B)
        o = _flash_out_sink(carry, sink_col)
        for hh in range(HPG):
            o_ref[:, (g * HPG + hh) * HEAD_DIM:(g * HPG + hh + 1) * HEAD_DIM] = o[hh * t_new:(hh + 1) * t_new]


def _swa_sample(h, row0, buf, lk, slopes, sinks, n_seq, t_new, past):
    buf_rows = buf.shape[2]
    smem = pl.BlockSpec(memory_space=pltpu.SMEM)
    return pl.pallas_call(
        functools.partial(_swa_sample_kernel, t_new, past),
        grid=(n_seq,),
        in_specs=[smem, smem,
                  pl.BlockSpec((t_new, Q_W), lambda b: (row0 // t_new + b, 0)),
                  pl.BlockSpec((t_new, KV_W), lambda b: (row0 // t_new + b, B_KV // KV_W)),
                  pl.BlockSpec((None, None, buf_rows, HEAD_DIM), lambda b: (lk, b, 0, 0))],
        out_specs=pl.BlockSpec((t_new, Q_W), lambda b: (b, 0)),
        out_shape=jax.ShapeDtypeStruct((n_seq * t_new, Q_W), F32),
        compiler_params=_params(("parallel",), 32),
        name="swa_sample",
    )(slopes, sinks, h, h, buf)


_STAIR = [PEER_TOPK // (k + 1) for k in range(PEER_TOPK)]
_STAIR_ROWS = -(-sum(_STAIR) // 8) * 8


def _top_values(work, out_ref):
    for k in range(PEER_TOPK):
        mx = jnp.max(work, axis=0, keepdims=True)
        out_ref[k:k + 1, :] = mx
        work = jnp.where(work == mx, -jnp.inf, work)


def _peer_route_kernel(x_ref, wqt_ref, sk_ref, xt_ref, b_ref, c_ref, ea_ref, b0_ref,
                       ta_ref, tb_ref, tc_ref, ts_ref, qt_ref):
    xt = x_ref[...].T.astype(BF16)
    xt_ref[...] = xt
    qt_ref[...] = _dot(wqt_ref[...], xt).astype(BF16)
    tc_ref[...] = jnp.full(tc_ref.shape, -jnp.inf, F32)
    n_col = x_ref.shape[0] // LANES

    for hd in range(PEER_HEADS):
        r0 = 2 * hd * PEER_HALF
        b_ref[hd] = _dot(sk_ref[2 * hd + 1], qt_ref[r0 + PEER_HALF:r0 + 2 * PEER_HALF])
        ea_ref[hd] = _dot(sk_ref[2 * hd], qt_ref[r0:r0 + PEER_HALF])

        def column(col, _, hd=hd):
            cs = pl.ds(pl.multiple_of(col * LANES, LANES), LANES)
            s_a = ea_ref[hd, :, cs]
            _top_values(s_a, ta_ref)
            _top_values(b_ref[hd, :, cs], tb_ref)
            top_b = tb_ref[...]
            off = 0
            for k, n_k in enumerate(_STAIR):
                tc_ref[off:off + n_k, :] = ta_ref[k:k + 1, :] + top_b[0:n_k]
                off += n_k
            _top_values(tc_ref[...], ts_ref)
            sc = ts_ref[...]
            thr = sc[PEER_TOPK - 1:PEER_TOPK]
            z = jnp.sum(jnp.exp(sc - sc[0:1]), axis=0, keepdims=True)
            cut_full = jnp.full(s_a.shape, jnp.inf, F32)
            for k, n_k in enumerate(_STAIR):
                a_k = ta_ref[k:k + 1, :]
                cut = jnp.min(jnp.where(a_k + top_b[0:n_k] >= thr, top_b[0:n_k], jnp.inf), axis=0, keepdims=True)
                cut_full = jnp.where(s_a == a_k, cut, cut_full)
            c_ref[hd, :, cs] = cut_full
            ea_ref[hd, :, cs] = jnp.exp(s_a - ta_ref[0:1, :]) / z
            b0_ref[hd:hd + 1, cs] = tb_ref[0:1, :]
            return 0

        lax.fori_loop(0, n_col, column, 0)


def _peer_route(x, wqt, sk, tb):
    n, d = x.shape
    big = jax.ShapeDtypeStruct((PEER_HEADS, PEER_KEYS, n), F32)
    big_spec = pl.BlockSpec((PEER_HEADS, PEER_KEYS, tb), lambda i: (0, 0, i))
    top = pltpu.VMEM((PEER_TOPK, LANES), F32)
    return pl.pallas_call(
        _peer_route_kernel,
        grid=(n // tb,),
        in_specs=[pl.BlockSpec((tb, d), lambda i: (i, 0)), pl.BlockSpec(wqt.shape, lambda i: (0, 0)),
                  pl.BlockSpec(sk.shape, lambda i: (0, 0, 0))],
        out_specs=[pl.BlockSpec((d, tb), lambda i: (0, i))] + [big_spec] * 3 + [
            pl.BlockSpec((PEER_HEADS, tb), lambda i: (0, i))],
        out_shape=[jax.ShapeDtypeStruct((d, n), BF16)] + [big] * 3 + [jax.ShapeDtypeStruct((PEER_HEADS, n), F32)],
        scratch_shapes=[top, top, pltpu.VMEM((_STAIR_ROWS, LANES), F32), top, pltpu.VMEM((wqt.shape[0], tb), BF16)],
        compiler_params=_params(("parallel",), 48),
        name="peer_route",
    )(x, wqt, sk)


def _peer_expert_kernel(alpha, ec, n_chunks, x_ref, xt_ref, b_ref, c_ref, ea_ref, b0_ref, u_ref, vt_ref, g_ref,
                        beta_ref, o_ref, acc_ref, eb_ref, act_a, act_b, hid_a, hid_b):
    j = pl.program_id(1)
    tb = x_ref.shape[0]
    per = ec // PEER_KEYS

    @pl.when(j == 0)
    def _():
        acc_ref[...] = jnp.zeros(acc_ref.shape, F32)
        for ref in (act_a, act_b, hid_a, hid_b):
            ref[...] = jnp.zeros(ref.shape, ref.dtype)
        for hd in range(PEER_HEADS):
            eb_ref[hd] = jnp.exp(b_ref[hd] - b0_ref[hd:hd + 1, :])

    def stages(act_w, act_r, hid_w, hid_r):
        k = j - 1
        valid = jnp.where(jnp.logical_and(k >= 0, k < n_chunks), 1.0, 0.0)
        i0 = jnp.clip(k, 0, n_chunks - 1) * per

        def act_piece(h):
            cols = slice(h * MXU_N, (h + 1) * MXU_N)
            act_w[:, cols] = _dot(u_ref[...], xt_ref[:, cols])

        def acc_piece(r):
            rows = slice(r * ACC_ROWS, (r + 1) * ACC_ROWS)
            acc_ref[rows, :] += _dot(vt_ref[rows, :], hid_r[...])

        c_rows = [[c_ref[hd, pl.ds(i0 + ii, 1), :] for hd in range(PEER_HEADS)] for ii in range(per)]
        ea_rows = [[ea_ref[hd, pl.ds(i0 + ii, 1), :] * valid for hd in range(PEER_HEADS)] for ii in range(per)]

        def gate_tiles(col):
            cs = slice(col * LANES, (col + 1) * LANES)
            for half in range(2):
                ks = slice(half * GATE_ROWS, (half + 1) * GATE_ROWS)
                w = [None] * per
                for hd in range(PEER_HEADS):
                    b = b_ref[hd, ks, cs]
                    eb = eb_ref[hd, ks, cs]
                    for ii in range(per):
                        t = jnp.where(b >= c_rows[ii][hd][:, cs], eb, 0.0) * ea_rows[ii][hd][:, cs]
                        w[ii] = t if w[ii] is None else w[ii] + t
                for ii in range(per):
                    rows = slice(ii * PEER_KEYS + half * GATE_ROWS, ii * PEER_KEYS + (half + 1) * GATE_ROWS)
                    act = act_r[rows, cs]
                    hid_w[rows, cs] = (0.5 * act * (1.0 + lax.erf(act * SQRT_HALF)) * w[ii]).astype(BF16)

        mxu = [functools.partial(act_piece, h) for h in range(tb // MXU_N)]
        mxu += [functools.partial(acc_piece, r) for r in range(acc_ref.shape[0] // ACC_ROWS)]
        n_col = tb // LANES
        for col in range(n_col):
            for piece in mxu[col * len(mxu) // n_col:(col + 1) * len(mxu) // n_col]:
                piece()
            gate_tiles(col)

    @pl.when(j % 2 == 0)
    def _():
        stages(act_a, act_b, hid_b, hid_a)

    @pl.when(j % 2 == 1)
    def _():
        stages(act_b, act_a, hid_a, hid_b)

    @pl.when(j == pl.num_programs(1) - 1)
    def _():
        o_ref[...] = _layer_norm(alpha * x_ref[...] + acc_ref[...].T, g_ref[...], beta_ref[...])


def _peer_experts(x, route, u, vt, layer, g, beta, alpha, tb, ec):
    n, d = x.shape
    ne = u.shape[1]
    xt, b, c, ea, b0 = route
    n_chunks = ne // ec
    big_spec = pl.BlockSpec((PEER_HEADS, PEER_KEYS, tb), lambda i, j: (0, 0, i))
    act = pltpu.VMEM((ec, tb), F32)
    hid = pltpu.VMEM((ec, tb), BF16)
    return pl.pallas_call(
        functools.partial(_peer_expert_kernel, alpha, ec, n_chunks),
        grid=(n // tb, n_chunks + 2),
        in_specs=[pl.BlockSpec((tb, d), lambda i, j: (i, 0)), pl.BlockSpec((d, tb), lambda i, j: (0, i))]
        + [big_spec] * 3 + [
            pl.BlockSpec((PEER_HEADS, tb), lambda i, j: (0, i)),
            pl.BlockSpec((None, ec, d), lambda i, j: (layer, jnp.minimum(j, n_chunks - 1), 0)),
            pl.BlockSpec((None, d, ec), lambda i, j: (layer, 0, jnp.clip(j - 2, 0, n_chunks - 1))),
            pl.BlockSpec((1, d), lambda i, j: (0, 0)), pl.BlockSpec((1, d), lambda i, j: (0, 0))],
        out_specs=pl.BlockSpec((tb, d), lambda i, j: (i, 0)),
        out_shape=jax.ShapeDtypeStruct((n, d), F32),
        scratch_shapes=[pltpu.VMEM((d, tb), F32), pltpu.VMEM((PEER_HEADS, PEER_KEYS, tb), F32), act, act, hid, hid],
        compiler_params=_params(("parallel", "arbitrary"), 56),
        name="peer_experts",
    )(x, xt, b, c, ea, b0, u, vt, g, beta)


def _overlap_table(n_cmp_pad, n_blk, n_lanes):
    c0 = jnp.arange(n_cmp_pad)[:, None] * CMP_STRIDE
    s0 = jnp.arange(n_lanes)[None, :] * SEL_BLK
    ov = (c0 <= s0 + SEL_BLK - 1) & (c0 + CMP_LEN - 1 >= s0) & (jnp.arange(n_lanes)[None, :] < n_blk)
    return ov.astype(BF16)


def _expand_table(n_lanes, n_pos):
    return (jnp.arange(n_lanes)[:, None] == jnp.arange(n_pos)[None, :] // SEL_BLK).astype(BF16)


def _alibi_slopes():
    i = jnp.arange(1, N_HEADS + 1, dtype=F32)
    return jnp.exp2(-8.0 * i / N_HEADS)


def _round_up(x, m):
    return -(-x // m) * m


def kernel(x_prompt, x_sample, mem_prompt, cache_a_cmp_kv, cache_a_sel_kv, cache_a_win_kv, cache_b_kv, cache_mem_kv, page_table, w_in_a, b_gate_a, w_cmp_a, pe_cmp_a, w_in_b, sinks_b, w_mem_kv, w_out, b_out, ln_g, ln_b, peer_wq, peer_subkeys, peer_u, peer_v):
    batch, seq, d = x_prompt.shape
    n_seq, t_new, _ = x_sample.shape
    depth = w_out.shape[0]
    past = page_table.shape[1] * PAGE_SIZE
    n_prompt = batch * seq
    n_tok = n_prompt + n_seq * t_new
    alpha = (2.0 * depth) ** 0.25
    assert past % SEL_BLK == 0 and t_new <= SEL_BLK and n_tok % TB_PEER == 0 and n_tok % TM_PROJ == 0
    tq = min(TQ_ATTN, seq)
    slopes = _alibi_slopes()

    x = jnp.concatenate([x_prompt.reshape(n_prompt, d), x_sample.reshape(n_seq * t_new, d)], axis=0)
    kv_shape = lambda n, t, g: (n, t, 2, g, HEAD_DIM)
    a_cmp_p, a_cmp_s, a_sel_p, a_sel_s, a_win_p, a_win_s, b_kv_p, b_kv_s, mem_p = ([] for _ in range(9))

    n_pool = cache_a_cmp_kv.shape[1]
    cmp_pages = cache_a_cmp_kv.reshape(-1, n_pool, PAGE_SIZE // CMP_STRIDE, CMP_STRIDE * KV_ROWS, HEAD_DIM)
    sel_pages = cache_a_sel_kv.reshape(-1, n_pool, PAGE_SIZE * KV_ROWS, HEAD_DIM)
    win_rows = cache_a_win_kv.reshape(cache_a_win_kv.shape[0], n_seq, -1, HEAD_DIM)
    buf_rows = cache_b_kv.reshape(cache_b_kv.shape[0], n_seq, -1, HEAD_DIM)
    mem_rows = cache_mem_kv.reshape(depth, n_seq, -1, HEAD_DIM)
    peer_ub = peer_u.astype(BF16)
    peer_vt = jnp.swapaxes(peer_v, 1, 2).astype(BF16)

    def shifted_rows(rows, new):
        new = new.reshape(n_seq, t_new * KV_ROWS, HEAD_DIM)
        out = jnp.concatenate([rows[:, t_new * KV_ROWS:], new], axis=1)
        return out.reshape(kv_shape(n_seq, rows.shape[1] // KV_ROWS, N_KV))

    for li in range(depth):
        lk = li // 2
        n_mem = mem_prompt.shape[1]
        mkv_p = _matmul(mem_prompt.reshape(-1, d), w_mem_kv[li].astype(BF16), batch * n_mem, 2 * QM_W)
        mem_p.append(mkv_p.reshape(batch, n_mem, 2, N_MEM_HEADS, HEAD_DIM))
        mkv_p = mkv_p.reshape(1, batch, n_mem, 2 * QM_W)

        if li % 2 == 0:
            w_in = w_in_a[lk]
            g0 = Q_W + 3 * KV_W
            gate_w = w_in[:, g0:g0 + 3 * N_HEADS].reshape(d, N_KV, 3 * HPG)
            gate_w = jnp.pad(gate_w, ((0, 0), (0, 0), (0, LANES - 3 * HPG))).reshape(d, N_KV * LANES)
            w_r = jnp.concatenate([w_in[:, :g0], w_in[:, g0 + 3 * N_HEADS:], gate_w], axis=1).astype(BF16)
            bg = jnp.pad(b_gate_a[lk].reshape(N_KV, 3 * HPG), ((0, 0), (0, LANES - 3 * HPG))).reshape(1, N_KV * LANES)
            h = _matmul(x, w_r, TM_PROJ, A_WIDTH // 6)

            w_cmp = w_cmp_a[lk].astype(BF16)
            wab = jnp.concatenate([w_cmp[:, :CMP_STRIDE], w_cmp[:, CMP_STRIDE:]], axis=-1)
            pe = pe_cmp_a[lk]
            hp = h[:n_prompt].reshape(batch, seq, -1)
            hs = h[n_prompt:].reshape(n_seq, t_new, -1)
            a_cmp_p.append(hp[..., A_KC:A_KC + KV_W].reshape(kv_shape(batch, seq, N_KV)))
            kvc_p = _compress_prompt(a_cmp_p[-1].reshape(-1, HEAD_DIM), wab, w_cmp, pe, batch, seq)
            n_blk_p = seq // SEL_BLK
            ov_p = _overlap_table(seq // CMP_STRIDE, n_blk_p, _round_up(n_blk_p, LANES))
            ex_p = _expand_table(_round_up(n_blk_p, LANES), seq)
            mix_p = _nsa_prompt(h, kvc_p, slopes, bg, ov_p, ex_p, batch, seq, tq)

            n_blk_s = -(-(past + t_new) // SEL_BLK)
            nbl = _round_up(n_blk_s, LANES)
            ov_s = _overlap_table(past // CMP_STRIDE, n_blk_s, nbl)
            ex_s = _expand_table(nbl, past)
            oc_s, sel_s = _nsa_sample_cmp(h, n_prompt, cmp_pages, lk, page_table, slopes, wab, w_cmp, pe, ov_s,
                                          n_seq, t_new, past)
            mix_s = _nsa_sample_sel(h, n_prompt, sel_pages, lk, page_table, slopes, sel_s, ex_s, oc_s, bg, win_rows,
                                    n_seq, t_new, past)

            a_cmp_s.append(hs[..., A_KC:A_KC + KV_W].reshape(kv_shape(n_seq, t_new, N_KV)))
            a_sel_p.append(hp[..., A_KS:A_KS + KV_W].reshape(kv_shape(batch, seq, N_KV)))
            a_sel_s.append(hs[..., A_KS:A_KS + KV_W].reshape(kv_shape(n_seq, t_new, N_KV)))
            wn = min(WIN_A, seq)
            a_win_p.append(hp[:, seq - wn:, A_KW:A_KW + KV_W].reshape(kv_shape(batch, wn, N_KV)))
            a_win_s.append(shifted_rows(win_rows[lk], hs[..., A_KW:A_KW + KV_W]))
            qm_col = A_QM
        else:
            h = _matmul(x, w_in_b[lk].astype(BF16), TM_PROJ, B_WIDTH // 4)
            mix_p = _swa_prompt(h, slopes, sinks_b[lk], batch, seq, tq)
            mix_s = _swa_sample(h, n_prompt, buf_rows, lk, slopes, sinks_b[lk], n_seq, t_new, past)
            hp = h[:n_prompt].reshape(batch, seq, -1)
            hs = h[n_prompt:].reshape(n_seq, t_new, -1)
            wn = min(WIN_B, seq)
            b_kv_p.append(hp[:, seq - wn:, B_KV:B_KV + KV_W].reshape(kv_shape(batch, wn, N_KV)))
            b_kv_s.append(shifted_rows(buf_rows[lk], hs[..., B_KV:B_KV + KV_W]))
            qm_col = B_QM

        mem_o_p = _mem_attend(h, qm_col, 0, mkv_p, 0, tq, seq // tq)
        mem_o_s = _mem_attend(h, qm_col, n_prompt, mem_rows, li, t_new, 1, math.gcd(n_seq, MEM_SEQS_PER_STEP))
        mix = jnp.concatenate([mix_p, mix_s], axis=0)
        mem_o = jnp.concatenate([mem_o_p, mem_o_s], axis=0)
        w_o = w_out[li].astype(BF16)
        x = _outproj_ln(mix, mem_o, w_o[:Q_W], w_o[Q_W:], b_out[li].reshape(1, d), x,
                        ln_g[li, 0].reshape(1, d), ln_b[li, 0].reshape(1, d), alpha, TM_OUT)

        sk = peer_subkeys[li].reshape(2 * PEER_HEADS, PEER_KEYS, PEER_HALF).astype(BF16)
        route = _peer_route(x, peer_wq[li].T.astype(BF16), sk, TB_PEER)
        x = _peer_experts(x, route, peer_ub, peer_vt, li, ln_g[li, 1].reshape(1, d), ln_b[li, 1].reshape(1, d),
                          alpha, TB_PEER, EC_PEER)

    return (x[:n_prompt].reshape(batch, seq, d), x[n_prompt:].reshape(n_seq, t_new, d),
            jnp.stack(a_cmp_p), jnp.stack(a_cmp_s), jnp.stack(a_sel_p), jnp.stack(a_sel_s),
            jnp.stack(a_win_p), jnp.stack(a_win_s), jnp.stack(b_kv_p), jnp.stack(b_kv_s), jnp.stack(mem_p))
```

```python
import functools
import math

import jax
import jax.numpy as jnp
from jax import lax
from jax.experimental import pallas as pl
from jax.experimental.pallas import tpu as pltpu

F32 = jnp.float32
BF16 = jnp.bfloat16

HEAD_DIM = 128
N_MEM_HEADS = 4
N_KV = 2
HPG = 6
N_HEADS = N_KV * HPG
Q_W = N_HEADS * HEAD_DIM
KV_W = 2 * N_KV * HEAD_DIM
KV_ROWS = 2 * N_KV
QM_W = N_MEM_HEADS * HEAD_DIM
CMP_STRIDE = 16
CMP_LEN = 32
SEL_BLK = 64
N_SEL = 16
WIN_A = 512
WIN_B = 128
PAGE_SIZE = 128
PEER_HEADS = 8
PEER_KEYS = 128
PEER_TOPK = 16
PEER_HALF = 128
LN_EPS = 1e-5
NEG_INF = -1e30
SEL_FORCE = 1e4
SCALE = HEAD_DIM ** -0.5
SQRT_HALF = math.sqrt(0.5)
LANES = 128
MXU_N = 256
ACC_ROWS = 256
GATE_ROWS = 64

TM_PROJ = 1024
TM_OUT = 512
TQ_ATTN = 256
TB_PEER = 512
EC_PEER = 512
PAGES_PER_STEP = 32
MEM_SEQS_PER_STEP = 8

A_KC, A_KS, A_KW, A_QM, A_GATE = Q_W, Q_W + KV_W, Q_W + 2 * KV_W, Q_W + 3 * KV_W, Q_W + 3 * KV_W + QM_W
A_WIDTH = A_GATE + N_KV * LANES
B_KV, B_QM = Q_W, Q_W + KV_W
B_WIDTH = B_QM + QM_W


def _params(sem, vmem_mb, flags=None):
    return pltpu.CompilerParams(dimension_semantics=sem, vmem_limit_bytes=vmem_mb << 20, flags=flags)


def _dot(a, b):
    return jnp.dot(a, b, preferred_element_type=F32)


def _dot_nt(a, b):
    return lax.dot_general(a, b, (((1,), (1,)), ((), ())), preferred_element_type=F32)


def _mm_kernel(x_ref, w_ref, o_ref):
    o_ref[...] = _dot(x_ref[...].astype(BF16), w_ref[...])


def _matmul(x, w, tm, tn):
    m, k = x.shape
    n = w.shape[1]
    return pl.pallas_call(
        _mm_kernel,
        grid=(m // tm, n // tn),
        in_specs=[pl.BlockSpec((tm, k), lambda i, j: (i, 0)), pl.BlockSpec((k, tn), lambda i, j: (0, j))],
        out_specs=pl.BlockSpec((tm, tn), lambda i, j: (i, j)),
        out_shape=jax.ShapeDtypeStruct((m, n), F32),
        compiler_params=_params(("parallel", "parallel"), 40),
        name="proj",
    )(x, w)


def _layer_norm(z, g, b):
    mu = jnp.mean(z, axis=-1, keepdims=True)
    zc = z - mu
    var = jnp.mean(zc * zc, axis=-1, keepdims=True)
    return zc * lax.rsqrt(var + LN_EPS) * g + b


def _outproj_kernel(alpha, mix_ref, mem_ref, w1_ref, w2_ref, bias_ref, x_ref, g_ref, b_ref, o_ref):
    y = _dot(mix_ref[...].astype(BF16), w1_ref[...]) + _dot(mem_ref[...].astype(BF16), w2_ref[...]) + bias_ref[...]
    o_ref[...] = _layer_norm(alpha * x_ref[...] + y, g_ref[...], b_ref[...])


def _outproj_ln(mix, mem, w1, w2, bias, x, g, b, alpha, tm):
    m, d = x.shape
    row = lambda i: (i, 0)
    fixed = lambda i: (0, 0)
    return pl.pallas_call(
        functools.partial(_outproj_kernel, alpha),
        grid=(m // tm,),
        in_specs=[pl.BlockSpec((tm, Q_W), row), pl.BlockSpec((tm, QM_W), row),
                  pl.BlockSpec((Q_W, d), fixed), pl.BlockSpec((QM_W, d), fixed), pl.BlockSpec((1, d), fixed),
                  pl.BlockSpec((tm, d), row), pl.BlockSpec((1, d), fixed), pl.BlockSpec((1, d), fixed)],
        out_specs=pl.BlockSpec((tm, d), row),
        out_shape=jax.ShapeDtypeStruct((m, d), F32),
        compiler_params=_params(("parallel",), 48),
        name="outproj_ln",
    )(mix, mem, w1, w2, bias, x, g, b)


def _flash_init(rows):
    return (jnp.full((rows, 1), NEG_INF, F32), jnp.zeros((rows, 1), F32), jnp.zeros((rows, HEAD_DIM), F32))


def _flash_tile(carry, q, k, v, slope_col, qpos_col, kpos_row, window=None, extra=None):
    m, l, acc = carry
    dist = qpos_col - kpos_row
    s = _dot_nt(q, k) * SCALE - slope_col * dist.astype(F32)
    mask = dist >= 0
    if window is not None:
        mask = jnp.logical_and(mask, dist < window)
    if extra is not None:
        mask = jnp.logical_and(mask, extra)
    s = jnp.where(mask, s, NEG_INF)
    m_new = jnp.maximum(m, jnp.max(s, axis=1, keepdims=True))
    e = jnp.where(mask, jnp.exp(s - m_new), 0.0)
    corr = jnp.exp(m - m_new)
    l = l * corr + jnp.sum(e, axis=1, keepdims=True)
    acc = acc * corr + _dot(e.astype(BF16), v)
    return m_new, l, acc


def _flash_out(carry):
    _, l, acc = carry
    return acc / jnp.maximum(l, 1e-30)


def _flash_out_sink(carry, sink_col):
    m, l, acc = carry
    m_f = jnp.maximum(m, sink_col)
    corr = jnp.exp(m - m_f)
    return acc * corr / (l * corr + jnp.exp(sink_col - m_f))


def _stack_heads(q_ref, nq):
    return jnp.concatenate([q_ref[:, hh * HEAD_DIM:(hh + 1) * HEAD_DIM] for hh in range(HPG)], axis=0).astype(BF16)


def _head_cols(ref, base, nq):
    return jnp.concatenate([jnp.full((nq, 1), ref[base + hh], F32) for hh in range(HPG)], axis=0)


def _stacked_pos(q0, nq):
    pos = q0 + lax.broadcasted_iota(jnp.int32, (nq, 1), 0)
    return jnp.concatenate([pos] * HPG, axis=0)


def _tile_rows(x):
    return jnp.concatenate([x] * HPG, axis=0)


def _top_blocks(score, lane):
    sel = jnp.zeros_like(score)
    work = score
    for _ in range(N_SEL):
        hit = lane == jnp.argmax(work, axis=1, keepdims=True).astype(jnp.int32)
        sel = jnp.where(hit, 1.0, sel)
        work = jnp.where(hit, -jnp.inf, work)
    return sel


def _cmp_branch(q_all, kc, vc, slope_col, qpos_col, ov_ref, nq, n_blk):
    ncp = kc.shape[0]
    end_row = lax.broadcasted_iota(jnp.int32, (1, ncp), 1) * CMP_STRIDE + (CMP_LEN - 1)
    dist = qpos_col - end_row
    mask = dist >= 0
    s = jnp.where(mask, _dot_nt(q_all, kc) * SCALE - slope_col * dist.astype(F32), NEG_INF)
    e = jnp.where(mask, jnp.exp(s - jnp.max(s, axis=1, keepdims=True)), 0.0)
    p = e / jnp.maximum(jnp.sum(e, axis=1, keepdims=True), 1e-30)
    o_c = _dot(p.astype(BF16), vc)
    imp_c = p[0:nq]
    for hh in range(1, HPG):
        imp_c = imp_c + p[hh * nq:(hh + 1) * nq]
    hi = imp_c.astype(BF16)
    lo = (imp_c - hi.astype(F32)).astype(BF16)
    imp = _dot(hi, ov_ref[...]) + _dot(lo, ov_ref[...])
    nl = imp.shape[1]
    lane_i = lax.broadcasted_iota(jnp.int32, (nq, nl), 1)
    cur = qpos_col[0:nq] // SEL_BLK
    forced = jnp.logical_or(lane_i == 0, jnp.logical_or(lane_i == cur, lane_i == cur - 1))
    score = jnp.where(forced, SEL_FORCE, jnp.where(lane_i <= cur, imp, -SEL_FORCE))
    score = jnp.where(lane_i < n_blk, score, -jnp.inf)
    return o_c, _top_blocks(score, lane_i)


def _gated_sum(gate_ref, bg_ref, branches, o_ref, nq):
    gates = jax.nn.sigmoid(gate_ref[...] + bg_ref[...])
    for hh in range(HPG):
        rows = slice(hh * nq, (hh + 1) * nq)
        o = gates[:, 3 * hh:3 * hh + 1] * branches[0][rows]
        for j in (1, 2):
            o = o + gates[:, 3 * hh + j:3 * hh + j + 1] * branches[j][rows]
        o_ref[:, hh * HEAD_DIM:(hh + 1) * HEAD_DIM] = o


def _chunk_ab(load_rows, wab_ref, c):
    acc = None
    for p in range(CMP_STRIDE // 2):
        x = jnp.concatenate([load_rows(2 * p), load_rows(2 * p + 1)], axis=1).astype(BF16)
        t = _dot(x, wab_ref[c, p])
        acc = t if acc is None else acc + t
    return acc


def _cmp_bias_kernel(pe_ref, w_ref, o_ref):
    for c in range(2):
        acc = jnp.zeros((8, HEAD_DIM), F32)
        for r in range(CMP_LEN):
            acc = acc + _dot(jnp.broadcast_to(pe_ref[c, r:r + 1, :], (8, HEAD_DIM)).astype(BF16), w_ref[c, r])
        o_ref[c] = acc


def _cmp_bias(pe, w):
    return pl.pallas_call(
        _cmp_bias_kernel,
        out_shape=jax.ShapeDtypeStruct((2, 8, HEAD_DIM), F32),
        name="cmp_bias",
    )(pe, w)


def _blocks_from_ab(ab, bias):
    n = ab.shape[0]
    return ab[:, :HEAD_DIM] + pltpu.roll(ab[:, HEAD_DIM:], n - 1, 0) + bias


def _compress_kernel(nch, kv_ref, wab_ref, bias_ref, o_ref):
    for c in range(2):
        for g in range(N_KV):
            cg = c * N_KV + g
            ab = _chunk_ab(lambda r: kv_ref[pl.ds(r * KV_ROWS + cg, nch, stride=CMP_STRIDE * KV_ROWS), :], wab_ref, c)
            o_ref[c, g] = _blocks_from_ab(ab, bias_ref[c, 0:1, :])


def _compress_prompt(kv, wab, bias, batch, seq):
    nch = seq // CMP_STRIDE
    return pl.pallas_call(
        functools.partial(_compress_kernel, nch),
        grid=(batch,),
        in_specs=[pl.BlockSpec((seq * KV_ROWS, HEAD_DIM), lambda b: (b, 0)),
                  pl.BlockSpec(wab.shape, lambda b: (0, 0, 0, 0)), pl.BlockSpec(bias.shape, lambda b: (0, 0, 0))],
        out_specs=pl.BlockSpec((None, 2, N_KV, nch, HEAD_DIM), lambda b: (b, 0, 0, 0, 0)),
        out_shape=jax.ShapeDtypeStruct((batch, 2, N_KV, nch, HEAD_DIM), F32),
        compiler_params=_params(("parallel",), 40),
        name="compress_prompt",
    )(kv, wab, bias)


def _nsa_prompt_kernel(tq, seq, slopes_ref, q_ref, gate_ref, bg_ref, kvc_ref, ks_ref, vs_ref, kw_ref, vw_ref,
                       ov_ref, e_ref, o_ref, selx_ref):
    g = pl.program_id(1)
    qi = pl.program_id(2)
    q0 = qi * tq
    q_all = _stack_heads(q_ref, tq)
    slope_col = _head_cols(slopes_ref, g * HPG, tq)
    qpos_col = _stacked_pos(q0, tq)

    o_c, sel = _cmp_branch(q_all, kvc_ref[0].astype(BF16), kvc_ref[1].astype(BF16), slope_col, qpos_col, ov_ref,
                           tq, seq // SEL_BLK)
    selx_ref[...] = _dot(sel.astype(BF16), e_ref[...])

    def key_tile(k_ref, v_ref, kj):
        k0 = pl.multiple_of(kj * tq, tq)
        kpos_row = k0 + lax.broadcasted_iota(jnp.int32, (1, tq), 1)
        return k0, k_ref[pl.ds(k0, tq), :].astype(BF16), v_ref[pl.ds(k0, tq), :].astype(BF16), kpos_row

    def sel_body(kj, carry):
        k0, k, v, kpos_row = key_tile(ks_ref, vs_ref, kj)
        extra = _tile_rows(selx_ref[:, pl.ds(k0, tq)]) > 0.5
        return _flash_tile(carry, q_all, k, v, slope_col, qpos_col, kpos_row, extra=extra)

    def win_body(kj, carry):
        _, k, v, kpos_row = key_tile(kw_ref, vw_ref, kj)
        return _flash_tile(carry, q_all, k, v, slope_col, qpos_col, kpos_row, window=WIN_A)

    rows = HPG * tq
    o_s = _flash_out(lax.fori_loop(0, qi + 1, sel_body, _flash_init(rows)))
    o_w = _flash_out(lax.fori_loop(jnp.maximum(qi - WIN_A // tq, 0), qi + 1, win_body, _flash_init(rows)))
    _gated_sum(gate_ref, bg_ref, (o_c, o_s, o_w), o_ref, tq)


def _nsa_prompt(h, kvc, slopes, bg, ov, expand, batch, seq, tq):
    nq = seq // tq
    nch = kvc.shape[3]
    kv_spec = lambda col: pl.BlockSpec((seq, HEAD_DIM), lambda b, g, i, col=col: (b, col // HEAD_DIM + g))
    return pl.pallas_call(
        functools.partial(_nsa_prompt_kernel, tq, seq),
        grid=(batch, N_KV, nq),
        in_specs=[pl.BlockSpec(memory_space=pltpu.SMEM),
                  pl.BlockSpec((tq, HPG * HEAD_DIM), lambda b, g, i: (b * nq + i, g)),
                  pl.BlockSpec((tq, LANES), lambda b, g, i: (b * nq + i, A_GATE // LANES + g)),
                  pl.BlockSpec((1, LANES), lambda b, g, i: (0, g)),
                  pl.BlockSpec((None, 2, None, nch, HEAD_DIM), lambda b, g, i: (b, 0, g, 0, 0)),
                  kv_spec(A_KS), kv_spec(A_KS + N_KV * HEAD_DIM), kv_spec(A_KW), kv_spec(A_KW + N_KV * HEAD_DIM),
                  pl.BlockSpec(ov.shape, lambda b, g, i: (0, 0)),
                  pl.BlockSpec(expand.shape, lambda b, g, i: (0, 0))],
        out_specs=pl.BlockSpec((tq, HPG * HEAD_DIM), lambda b, g, i: (b * nq + i, g)),
        out_shape=jax.ShapeDtypeStruct((batch * seq, Q_W), F32),
        scratch_shapes=[pltpu.VMEM((tq, seq), F32)],
        compiler_params=_params(("parallel", "parallel", "arbitrary"), 48),
        name="nsa_prompt",
    )(slopes, h, h, bg, kvc, h, h, h, h, ov, expand)


def _swa_prompt_kernel(tq, slopes_ref, sinks_ref, q_ref, k_ref, v_ref, o_ref):
    g = pl.program_id(1)
    qi = pl.program_id(2)
    q_all = _stack_heads(q_ref, tq)
    slope_col = _head_cols(slopes_ref, g * HPG, tq)
    sink_col = _head_cols(sinks_ref, g * HPG, tq)
    qpos_col = _stacked_pos(qi * tq, tq)

    def body(kj, carry):
        k0 = pl.multiple_of(kj * tq, tq)
        kpos_row = k0 + lax.broadcasted_iota(jnp.int32, (1, tq), 1)
        k = k_ref[pl.ds(k0, tq), :].astype(BF16)
        v = v_ref[pl.ds(k0, tq), :].astype(BF16)
        return _flash_tile(carry, q_all, k, v, slope_col, qpos_col, kpos_row, window=WIN_B)

    lo = jnp.maximum(qi - (WIN_B + tq - 1) // tq, 0)
    o = _flash_out_sink(lax.fori_loop(lo, qi + 1, body, _flash_init(HPG * tq)), sink_col)
    for hh in range(HPG):
        o_ref[:, hh * HEAD_DIM:(hh + 1) * HEAD_DIM] = o[hh * tq:(hh + 1) * tq]


def _swa_prompt(h, slopes, sinks, batch, seq, tq):
    nq = seq // tq
    kv_spec = lambda col: pl.BlockSpec((seq, HEAD_DIM), lambda b, g, i, col=col: (b, col // HEAD_DIM + g))
    smem = pl.BlockSpec(memory_space=pltpu.SMEM)
    return pl.pallas_call(
        functools.partial(_swa_prompt_kernel, tq),
        grid=(batch, N_KV, nq),
        in_specs=[smem, smem,
                  pl.BlockSpec((tq, HPG * HEAD_DIM), lambda b, g, i: (b * nq + i, g)),
                  kv_spec(B_KV), kv_spec(B_KV + N_KV * HEAD_DIM)],
        out_specs=pl.BlockSpec((tq, HPG * HEAD_DIM), lambda b, g, i: (b * nq + i, g)),
        out_shape=jax.ShapeDtypeStruct((batch * seq, Q_W), F32),
        compiler_params=_params(("parallel", "parallel", "arbitrary"), 40),
        name="swa_prompt",
    )(slopes, sinks, h, h, h)


def _mem_kernel(row_layout, q_ref, kv_ref, o_ref):
    n_sub = kv_ref.shape[0]
    tq = q_ref.shape[0] // n_sub
    for sub in range(n_sub):
        rows = slice(sub * tq, (sub + 1) * tq)
        for hd in range(N_MEM_HEADS):
            cols = slice(hd * HEAD_DIM, (hd + 1) * HEAD_DIM)
            if row_layout:
                n_mem = kv_ref.shape[1] // (2 * N_MEM_HEADS)
                k = kv_ref[sub, pl.ds(hd, n_mem, stride=2 * N_MEM_HEADS), :].astype(BF16)
                v = kv_ref[sub, pl.ds(N_MEM_HEADS + hd, n_mem, stride=2 * N_MEM_HEADS), :].astype(BF16)
            else:
                k = kv_ref[sub, :, hd * HEAD_DIM:(hd + 1) * HEAD_DIM].astype(BF16)
                v = kv_ref[sub, :, QM_W + hd * HEAD_DIM:QM_W + (hd + 1) * HEAD_DIM].astype(BF16)
            s = _dot_nt(q_ref[rows, cols].astype(BF16), k) * SCALE
            e = jnp.exp(s - jnp.max(s, axis=1, keepdims=True))
            p = e / jnp.sum(e, axis=1, keepdims=True)
            o_ref[rows, cols] = _dot(p.astype(BF16), v)


def _mem_attend(h, qm_col, row0, mkv, layer, tq, blocks_per_batch, n_sub=1):
    _, n_batch, kv_rows, kv_cols = mkv.shape
    assert n_sub == 1 or blocks_per_batch == 1
    nblk = n_batch * blocks_per_batch // n_sub
    rows = tq * n_sub
    return pl.pallas_call(
        functools.partial(_mem_kernel, kv_cols == HEAD_DIM),
        grid=(nblk,),
        in_specs=[pl.BlockSpec((rows, QM_W), lambda i: (row0 // rows + i, qm_col // QM_W)),
                  pl.BlockSpec((None, n_sub, kv_rows, kv_cols), lambda i: (layer, i // blocks_per_batch, 0, 0))],
        out_specs=pl.BlockSpec((rows, QM_W), lambda i: (i, 0)),
        out_shape=jax.ShapeDtypeStruct((nblk * rows, QM_W), F32),
        compiler_params=_params(("parallel",), 32),
        name="mem_attend",
    )(h, mkv)


def _nsa_sample_cmp_kernel(t_new, past, *refs):
    npg = PAGES_PER_STEP
    _, slopes_ref = refs[0], refs[1]
    page_refs = refs[2:2 + npg]
    q_ref, wab_ref, bias_ref, ov_ref, oc_ref, sel_ref, ab_ref = refs[2 + npg:]
    j = pl.program_id(1)
    rows_per_step = npg * PAGE_SIZE // CMP_STRIDE
    pages = [jnp.swapaxes(p[...], 0, 1) for p in page_refs]
    for c in range(2):
        for g in range(N_KV):
            cg = c * N_KV + g
            ab = _chunk_ab(lambda r: jnp.concatenate([x[r * KV_ROWS + cg] for x in pages], axis=0), wab_ref, c)
            ab_ref[cg, pl.ds(pl.multiple_of(j * rows_per_step, rows_per_step), rows_per_step), :] = ab

    @pl.when(j == pl.num_programs(1) - 1)
    def _():
        bias = [bias_ref[c, 0:1, :] for c in range(2)]
        n_blk = -(-(past + t_new) // SEL_BLK)
        for g in range(N_KV):
            kc = _blocks_from_ab(ab_ref[g], bias[0]).astype(BF16)
            vc = _blocks_from_ab(ab_ref[N_KV + g], bias[1]).astype(BF16)
            q_all = _stack_heads(q_ref.at[:, g * HPG * HEAD_DIM:(g + 1) * HPG * HEAD_DIM], t_new)
            slope_col = _head_cols(slopes_ref, g * HPG, t_new)
            qpos_col = _stacked_pos(past, t_new)
            o_c, sel = _cmp_branch(q_all, kc, vc, slope_col, qpos_col, ov_ref, t_new, n_blk)
            for hh in range(HPG):
                oc_ref[:, (g * HPG + hh) * HEAD_DIM:(g * HPG + hh + 1) * HEAD_DIM] = o_c[hh * t_new:(hh + 1) * t_new]
            sel_ref[g] = sel


def _nsa_sample_cmp(h, row0, cache, lk, page_table, slopes, wab, bias, ov, n_seq, t_new, past):
    n_pages = past // PAGE_SIZE
    npg = PAGES_PER_STEP
    nch = past // CMP_STRIDE
    page_spec = lambda k: pl.BlockSpec((None, None, PAGE_SIZE // CMP_STRIDE, CMP_STRIDE * KV_ROWS, HEAD_DIM),
                                       lambda b, j, pt, k=k: (lk, pt[b, j * npg + k], 0, 0, 0))
    fixed = lambda nd: (lambda b, j, pt: (0,) * nd)
    grid_spec = pltpu.PrefetchScalarGridSpec(
        num_scalar_prefetch=1,
        grid=(n_seq, n_pages // npg),
        in_specs=[pl.BlockSpec(memory_space=pltpu.SMEM)] + [page_spec(k) for k in range(npg)] + [
            pl.BlockSpec((t_new, Q_W), lambda b, j, pt: (row0 // t_new + b, 0)),
            pl.BlockSpec(wab.shape, fixed(4)), pl.BlockSpec(bias.shape, fixed(3)), pl.BlockSpec(ov.shape, fixed(2))],
        out_specs=[pl.BlockSpec((t_new, Q_W), lambda b, j, pt: (b, 0)),
                   pl.BlockSpec((None, N_KV, t_new, ov.shape[1]), lambda b, j, pt: (b, 0, 0, 0))],
        scratch_shapes=[pltpu.VMEM((2 * N_KV, nch, 2 * HEAD_DIM), F32)],
    )
    return pl.pallas_call(
        functools.partial(_nsa_sample_cmp_kernel, t_new, past),
        grid_spec=grid_spec,
        out_shape=[jax.ShapeDtypeStruct((n_seq * t_new, Q_W), F32),
                   jax.ShapeDtypeStruct((n_seq, N_KV, t_new, ov.shape[1]), F32)],
        compiler_params=_params(("parallel", "arbitrary"), 48),
        name="nsa_sample_cmp",
    )(page_table, slopes, *([cache] * npg), h, wab, bias, ov)


def _kv_rows(ref, c, g):
    return ref[pl.ds(c * N_KV + g, ref.shape[0] // KV_ROWS, stride=KV_ROWS), :].astype(BF16)


def _pad_keys(new_ref, col, t_new):
    rows = new_ref[:, col:col + HEAD_DIM]
    return jnp.concatenate([rows, jnp.zeros((LANES - t_new, HEAD_DIM), F32)], axis=0).astype(BF16)


def _nsa_sample_sel_kernel(t_new, past, *refs):
    npg = PAGES_PER_STEP
    slopes_ref = refs[1]
    page_refs = refs[2:2 + npg]
    (q_ref, sel_ref, e_ref, oc_ref, ks_new_ref, kw_new_ref, gate_ref, bg_ref, win_ref,
     o_ref, m_ref, l_ref, acc_ref) = refs[2 + npg:]
    j = pl.program_id(1)
    rows = HPG * t_new
    keys = npg * PAGE_SIZE
    qpos_col = _stacked_pos(past, t_new)

    @pl.when(j == 0)
    def _():
        m_ref[...] = jnp.full(m_ref.shape, NEG_INF, F32)
        l_ref[...] = jnp.zeros(l_ref.shape, F32)
        acc_ref[...] = jnp.zeros(acc_ref.shape, F32)

    kpos_row = j * keys + lax.broadcasted_iota(jnp.int32, (1, keys), 1)
    for g in range(N_KV):
        q_all = _stack_heads(q_ref.at[:, g * HPG * HEAD_DIM:(g + 1) * HPG * HEAD_DIM], t_new)
        slope_col = _head_cols(slopes_ref, g * HPG, t_new)
        k = jnp.concatenate([_kv_rows(p, 0, g) for p in page_refs], axis=0)
        v = jnp.concatenate([_kv_rows(p, 1, g) for p in page_refs], axis=0)
        extra = _tile_rows(_dot(sel_ref[g].astype(BF16), e_ref[...])) > 0.5
        carry = _flash_tile((m_ref[g], l_ref[g], acc_ref[g]), q_all, k, v, slope_col, qpos_col, kpos_row, extra=extra)
        m_ref[g], l_ref[g], acc_ref[g] = carry

    @pl.when(j == pl.num_programs(1) - 1)
    def _():
        new_pos = past + lax.broadcasted_iota(jnp.int32, (1, LANES), 1)
        wb = win_ref.shape[0] // KV_ROWS
        win_pos = past - wb + lax.broadcasted_iota(jnp.int32, (1, wb), 1)
        for g in range(N_KV):
            q_all = _stack_heads(q_ref.at[:, g * HPG * HEAD_DIM:(g + 1) * HPG * HEAD_DIM], t_new)
            slope_col = _head_cols(slopes_ref, g * HPG, t_new)
            sel_new = _tile_rows(sel_ref[g][:, past // SEL_BLK:past // SEL_BLK + 1]) > 0.5
            carry = _flash_tile((m_ref[g], l_ref[g], acc_ref[g]), q_all,
                                _pad_keys(ks_new_ref, g * HEAD_DIM, t_new),
                                _pad_keys(ks_new_ref, (N_KV + g) * HEAD_DIM, t_new),
                                slope_col, qpos_col, new_pos, extra=sel_new)
            o_s = _flash_out(carry)
            carry = _flash_tile(_flash_init(rows), q_all, _kv_rows(win_ref, 0, g), _kv_rows(win_ref, 1, g),
                                slope_col, qpos_col, win_pos, window=WIN_A)
            carry = _flash_tile(carry, q_all, _pad_keys(kw_new_ref, g * HEAD_DIM, t_new),
                                _pad_keys(kw_new_ref, (N_KV + g) * HEAD_DIM, t_new),
                                slope_col, qpos_col, new_pos, window=WIN_A)
            o_w = _flash_out(carry)
            o_c = jnp.concatenate([oc_ref[:, (g * HPG + hh) * HEAD_DIM:(g * HPG + hh + 1) * HEAD_DIM]
                                   for hh in range(HPG)], axis=0)
            _gated_sum(gate_ref.at[:, g * LANES:(g + 1) * LANES], bg_ref.at[:, g * LANES:(g + 1) * LANES],
                       (o_c, o_s, o_w), o_ref.at[:, g * HPG * HEAD_DIM:(g + 1) * HPG * HEAD_DIM], t_new)


def _nsa_sample_sel(h, row0, cache, lk, page_table, slopes, sel, expand, oc, bg, win, n_seq, t_new, past):
    n_pages = past // PAGE_SIZE
    npg = PAGES_PER_STEP
    nbl = sel.shape[3]
    win_rows = win.shape[2]
    rows = HPG * t_new
    page_spec = lambda k: pl.BlockSpec((None, None, PAGE_SIZE * KV_ROWS, HEAD_DIM),
                                       lambda b, j, pt, k=k: (lk, pt[b, j * npg + k], 0, 0))
    hrow = lambda width, col: pl.BlockSpec((t_new, width), lambda b, j, pt: (row0 // t_new + b, col // width))
    grid_spec = pltpu.PrefetchScalarGridSpec(
        num_scalar_prefetch=1,
        grid=(n_seq, n_pages // npg),
        in_specs=[pl.BlockSpec(memory_space=pltpu.SMEM)] + [page_spec(k) for k in range(npg)] + [
            hrow(Q_W, 0),
            pl.BlockSpec((None, N_KV, t_new, nbl), lambda b, j, pt: (b, 0, 0, 0)),
            pl.BlockSpec((nbl, npg * PAGE_SIZE), lambda b, j, pt: (0, j)),
            pl.BlockSpec((t_new, Q_W), lambda b, j, pt: (b, 0)),
            hrow(KV_W, A_KS), hrow(KV_W, A_KW), hrow(N_KV * LANES, A_GATE),
            pl.BlockSpec((1, N_KV * LANES), lambda b, j, pt: (0, 0)),
            pl.BlockSpec((None, None, win_rows, HEAD_DIM), lambda b, j, pt: (lk, b, 0, 0))],
        out_specs=pl.BlockSpec((t_new, Q_W), lambda b, j, pt: (b, 0)),
        scratch_shapes=[pltpu.VMEM((N_KV, rows, 1), F32), pltpu.VMEM((N_KV, rows, 1), F32),
                        pltpu.VMEM((N_KV, rows, HEAD_DIM), F32)],
    )
    return pl.pallas_call(
        functools.partial(_nsa_sample_sel_kernel, t_new, past),
        grid_spec=grid_spec,
        out_shape=jax.ShapeDtypeStruct((n_seq * t_new, Q_W), F32),
        compiler_params=_params(("parallel", "arbitrary"), 48),
        name="nsa_sample_sel",
    )(page_table, slopes, *([cache] * npg), h, sel, expand, oc, h, h, h, bg, win)


def _swa_sample_kernel(t_new, past, slopes_ref, sinks_ref, q_ref, new_ref, buf_ref, o_ref):
    wb = buf_ref.shape[0] // KV_ROWS
    qpos_col = _stacked_pos(past, t_new)
    buf_pos = past - wb + lax.broadcasted_iota(jnp.int32, (1, wb), 1)
    new_pos = past + lax.broadcasted_iota(jnp.int32, (1, LANES), 1)
    for g in range(N_KV):
        q_all = _stack_heads(q_ref.at[:, g * HPG * HEAD_DIM:(g + 1) * HPG * HEAD_DIM], t_new)
        slope_col = _head_cols(slopes_ref, g * HPG, t_new)
        sink_col = _head_cols(sinks_ref, g * HPG, t_new)
        carry = _flash_tile(_flash_init(HPG * t_new), q_all, _kv_rows(buf_ref, 0, g), _kv_rows(buf_ref, 1, g),
                            slope_col, qpos_col, buf_pos, window=WIN_B)
        carry = _flash_tile(carry, q_all, _pad_keys(new_ref, g * HEAD_DIM, t_new),
                            _pad_keys(new_ref, (N_KV + g) * HEAD_DIM, t_new),
                            slope_col, qpos_col, new_pos, window=WIN_B)
        o = _flash_out_sink(carry, sink_col)
        for hh in range(HPG):
            o_ref[:, (g * HPG + hh) * HEAD_DIM:(g * HPG + hh + 1) * HEAD_DIM] = o[hh * t_new:(hh + 1) * t_new]


def _swa_sample(h, row0, buf, lk, slopes, sinks, n_seq, t_new, past):
    buf_rows = buf.shape[2]
    smem = pl.BlockSpec(memory_space=pltpu.SMEM)
    return pl.pallas_call(
        functools.partial(_swa_sample_kernel, t_new, past),
        grid=(n_seq,),
        in_specs=[smem, smem,
                  pl.BlockSpec((t_new, Q_W), lambda b: (row0 // t_new + b, 0)),
                  pl.BlockSpec((t_new, KV_W), lambda b: (row0 // t_new + b, B_KV // KV_W)),
                  pl.BlockSpec((None, None, buf_rows, HEAD_DIM), lambda b: (lk, b, 0, 0))],
        out_specs=pl.BlockSpec((t_new, Q_W), lambda b: (b, 0)),
        out_shape=jax.ShapeDtypeStruct((n_seq * t_new, Q_W), F32),
        compiler_params=_params(("parallel",), 32),
        name="swa_sample",
    )(slopes, sinks, h, h, buf)


_STAIR = [PEER_TOPK // (k + 1) for k in range(PEER_TOPK)]
_STAIR_ROWS = -(-sum(_STAIR) // 8) * 8


def _top_values(work, out_ref):
    for k in range(PEER_TOPK):
        mx = jnp.max(work, axis=0, keepdims=True)
        out_ref[k:k + 1, :] = mx
        work = jnp.where(work == mx, -jnp.inf, work)


def _peer_route_kernel(x_ref, wqt_ref, sk_ref, xt_ref, b_ref, c_ref, ea_ref, b0_ref,
                       ta_ref, tb_ref, tc_ref, ts_ref, qt_ref):
    xt = x_ref[...].T.astype(BF16)
    xt_ref[...] = xt
    qt_ref[...] = _dot(wqt_ref[...], xt).astype(BF16)
    tc_ref[...] = jnp.full(tc_ref.shape, -jnp.inf, F32)
    n_col = x_ref.shape[0] // LANES

    for hd in range(PEER_HEADS):
        r0 = 2 * hd * PEER_HALF
        b_ref[hd] = _dot(sk_ref[2 * hd + 1], qt_ref[r0 + PEER_HALF:r0 + 2 * PEER_HALF])
        ea_ref[hd] = _dot(sk_ref[2 * hd], qt_ref[r0:r0 + PEER_HALF])

        def column(col, _, hd=hd):
            cs = pl.ds(pl.multiple_of(col * LANES, LANES), LANES)
            s_a = ea_ref[hd, :, cs]
            _top_values(s_a, ta_ref)
            _top_values(b_ref[hd, :, cs], tb_ref)
            top_b = tb_ref[...]
            off = 0
            for k, n_k in enumerate(_STAIR):
                tc_ref[off:off + n_k, :] = ta_ref[k:k + 1, :] + top_b[0:n_k]
                off += n_k
            _top_values(tc_ref[...], ts_ref)
            sc = ts_ref[...]
            thr = sc[PEER_TOPK - 1:PEER_TOPK]
            z = jnp.sum(jnp.exp(sc - sc[0:1]), axis=0, keepdims=True)
            cut_full = jnp.full(s_a.shape, jnp.inf, F32)
            for k, n_k in enumerate(_STAIR):
                a_k = ta_ref[k:k + 1, :]
                cut = jnp.min(jnp.where(a_k + top_b[0:n_k] >= thr, top_b[0:n_k], jnp.inf), axis=0, keepdims=True)
                cut_full = jnp.where(s_a == a_k, cut, cut_full)
            c_ref[hd, :, cs] = cut_full
            ea_ref[hd, :, cs] = jnp.exp(s_a - ta_ref[0:1, :]) / z
            b0_ref[hd:hd + 1, cs] = tb_ref[0:1, :]
            return 0

        lax.fori_loop(0, n_col, column, 0)


def _peer_route(x, wqt, sk, tb):
    n, d = x.shape
    big = jax.ShapeDtypeStruct((PEER_HEADS, PEER_KEYS, n), F32)
    big_spec = pl.BlockSpec((PEER_HEADS, PEER_KEYS, tb), lambda i: (0, 0, i))
    top = pltpu.VMEM((PEER_TOPK, LANES), F32)
    return pl.pallas_call(
        _peer_route_kernel,
        grid=(n // tb,),
        in_specs=[pl.BlockSpec((tb, d), lambda i: (i, 0)), pl.BlockSpec(wqt.shape, lambda i: (0, 0)),
                  pl.BlockSpec(sk.shape, lambda i: (0, 0, 0))],
        out_specs=[pl.BlockSpec((d, tb), lambda i: (0, i))] + [big_spec] * 3 + [
            pl.BlockSpec((PEER_HEADS, tb), lambda i: (0, i))],
        out_shape=[jax.ShapeDtypeStruct((d, n), BF16)] + [big] * 3 + [jax.ShapeDtypeStruct((PEER_HEADS, n), F32)],
        scratch_shapes=[top, top, pltpu.VMEM((_STAIR_ROWS, LANES), F32), top, pltpu.VMEM((wqt.shape[0], tb), BF16)],
        compiler_params=_params(("parallel",), 48),
        name="peer_route",
    )(x, wqt, sk)


def _peer_expert_kernel(alpha, ec, n_chunks, x_ref, xt_ref, b_ref, c_ref, ea_ref, b0_ref, u_ref, vt_ref, g_ref,
                        beta_ref, o_ref, acc_ref, eb_ref, act_a, act_b, hid_a, hid_b):
    j = pl.program_id(1)
    tb = x_ref.shape[0]
    per = ec // PEER_KEYS

    @pl.when(j == 0)
    def _():
        acc_ref[...] = jnp.zeros(acc_ref.shape, F32)
        for ref in (act_a, act_b, hid_a, hid_b):
            ref[...] = jnp.zeros(ref.shape, ref.dtype)
        for hd in range(PEER_HEADS):
            eb_ref[hd] = jnp.exp(b_ref[hd] - b0_ref[hd:hd + 1, :])

    def stages(act_w, act_r, hid_w, hid_r):
        k = j - 1
        valid = jnp.where(jnp.logical_and(k >= 0, k < n_chunks), 1.0, 0.0)
        i0 = jnp.clip(k, 0, n_chunks - 1) * per

        def act_piece(h):
            cols = slice(h * MXU_N, (h + 1) * MXU_N)
            act_w[:, cols] = _dot(u_ref[...], xt_ref[:, cols])

        def acc_piece(r):
            rows = slice(r * ACC_ROWS, (r + 1) * ACC_ROWS)
            acc_ref[rows, :] += _dot(vt_ref[rows, :], hid_r[...])

        c_rows = [[c_ref[hd, pl.ds(i0 + ii, 1), :] for hd in range(PEER_HEADS)] for ii in range(per)]
        ea_rows = [[ea_ref[hd, pl.ds(i0 + ii, 1), :] * valid for hd in range(PEER_HEADS)] for ii in range(per)]

        def gate_tile(col, half):
            cs = slice(col * LANES, (col + 1) * LANES)
            ks = slice(half * GATE_ROWS, (half + 1) * GATE_ROWS)
            w = [None] * per
            for hd in range(PEER_HEADS):
                b = b_ref[hd, ks, cs]
                eb = eb_ref[hd, ks, cs]
                for ii in range(per):
                    t = jnp.where(b >= c_rows[ii][hd][:, cs], eb, 0.0) * ea_rows[ii][hd][:, cs]
                    w[ii] = t if w[ii] is None else w[ii] + t
            for ii in range(per):
                rows = slice(ii * PEER_KEYS + half * GATE_ROWS, ii * PEER_KEYS + (half + 1) * GATE_ROWS)
                act = act_r[rows, cs]
                hid_w[rows, cs] = (0.5 * act * (1.0 + lax.erf(act * SQRT_HALF)) * w[ii]).astype(BF16)

        mxu = [functools.partial(act_piece, h) for h in range(tb // MXU_N)]
        mxu += [functools.partial(acc_piece, r) for r in range(acc_ref.shape[0] // ACC_ROWS)]
        tiles = [(col, half) for col in range(tb // LANES) for half in range(PEER_KEYS // GATE_ROWS)]
        for t, (col, half) in enumerate(tiles):
            for piece in mxu[t * len(mxu) // len(tiles):(t + 1) * len(mxu) // len(tiles)]:
                piece()
            gate_tile(col, half)

    @pl.when(j % 2 == 0)
    def _():
        stages(act_a, act_b, hid_b, hid_a)

    @pl.when(j % 2 == 1)
    def _():
        stages(act_b, act_a, hid_a, hid_b)

    @pl.when(j == pl.num_programs(1) - 1)
    def _():
        o_ref[...] = _layer_norm(alpha * x_ref[...] + acc_ref[...].T, g_ref[...], beta_ref[...])


def _peer_experts(x, route, u, vt, layer, g, beta, alpha, tb, ec):
    n, d = x.shape
    ne = u.shape[1]
    xt, b, c, ea, b0 = route
    n_chunks = ne // ec
    big_spec = pl.BlockSpec((PEER_HEADS, PEER_KEYS, tb), lambda i, j: (0, 0, i))
    act = pltpu.VMEM((ec, tb), F32)
    hid = pltpu.VMEM((ec, tb), BF16)
    return pl.pallas_call(
        functools.partial(_peer_expert_kernel, alpha, ec, n_chunks),
        grid=(n // tb, n_chunks + 2),
        in_specs=[pl.BlockSpec((tb, d), lambda i, j: (i, 0)), pl.BlockSpec((d, tb), lambda i, j: (0, i))]
        + [big_spec] * 3 + [
            pl.BlockSpec((PEER_HEADS, tb), lambda i, j: (0, i)),
            pl.BlockSpec((None, ec, d), lambda i, j: (layer, jnp.minimum(j, n_chunks - 1), 0)),
            pl.BlockSpec((None, d, ec), lambda i, j: (layer, 0, jnp.clip(j - 2, 0, n_chunks - 1))),
            pl.BlockSpec((1, d), lambda i, j: (0, 0)), pl.BlockSpec((1, d), lambda i, j: (0, 0))],
        out_specs=pl.BlockSpec((tb, d), lambda i, j: (i, 0)),
        out_shape=jax.ShapeDtypeStruct((n, d), F32),
        scratch_shapes=[pltpu.VMEM((d, tb), F32), pltpu.VMEM((PEER_HEADS, PEER_KEYS, tb), F32), act, act, hid, hid],
        compiler_params=_params(("parallel", "arbitrary"), 56),
        name="peer_experts",
    )(x, xt, b, c, ea, b0, u, vt, g, beta)


def _overlap_table(n_cmp_pad, n_blk, n_lanes):
    c0 = jnp.arange(n_cmp_pad)[:, None] * CMP_STRIDE
    s0 = jnp.arange(n_lanes)[None, :] * SEL_BLK
    ov = (c0 <= s0 + SEL_BLK - 1) & (c0 + CMP_LEN - 1 >= s0) & (jnp.arange(n_lanes)[None, :] < n_blk)
    return ov.astype(BF16)


def _expand_table(n_lanes, n_pos):
    return (jnp.arange(n_lanes)[:, None] == jnp.arange(n_pos)[None, :] // SEL_BLK).astype(BF16)


def _alibi_slopes():
    i = jnp.arange(1, N_HEADS + 1, dtype=F32)
    return jnp.exp2(-8.0 * i / N_HEADS)


def _round_up(x, m):
    return -(-x // m) * m


def kernel(x_prompt, x_sample, mem_prompt, cache_a_cmp_kv, cache_a_sel_kv, cache_a_win_kv, cache_b_kv, cache_mem_kv, page_table, w_in_a, b_gate_a, w_cmp_a, pe_cmp_a, w_in_b, sinks_b, w_mem_kv, w_out, b_out, ln_g, ln_b, peer_wq, peer_subkeys, peer_u, peer_v):
    batch, seq, d = x_prompt.shape
    n_seq, t_new, _ = x_sample.shape
    depth = w_out.shape[0]
    past = page_table.shape[1] * PAGE_SIZE
    n_prompt = batch * seq
    n_tok = n_prompt + n_seq * t_new
    alpha = (2.0 * depth) ** 0.25
    assert past % SEL_BLK == 0 and t_new <= SEL_BLK and n_tok % TB_PEER == 0 and n_tok % TM_PROJ == 0
    tq = min(TQ_ATTN, seq)
    slopes = _alibi_slopes()

    x = jnp.concatenate([x_prompt.reshape(n_prompt, d), x_sample.reshape(n_seq * t_new, d)], axis=0)
    kv_shape = lambda n, t, g: (n, t, 2, g, HEAD_DIM)
    a_cmp_p, a_cmp_s, a_sel_p, a_sel_s, a_win_p, a_win_s, b_kv_p, b_kv_s, mem_p = ([] for _ in range(9))

    n_pool = cache_a_cmp_kv.shape[1]
    cmp_pages = cache_a_cmp_kv.reshape(-1, n_pool, PAGE_SIZE // CMP_STRIDE, CMP_STRIDE * KV_ROWS, HEAD_DIM)
    sel_pages = cache_a_sel_kv.reshape(-1, n_pool, PAGE_SIZE * KV_ROWS, HEAD_DIM)
    win_rows = cache_a_win_kv.reshape(cache_a_win_kv.shape[0], n_seq, -1, HEAD_DIM)
    buf_rows = cache_b_kv.reshape(cache_b_kv.shape[0], n_seq, -1, HEAD_DIM)
    mem_rows = cache_mem_kv.reshape(depth, n_seq, -1, HEAD_DIM)
    peer_ub = peer_u.astype(BF16)
    peer_vt = jnp.swapaxes(peer_v, 1, 2).astype(BF16)

    def shifted_rows(rows, new):
        new = new.reshape(n_seq, t_new * KV_ROWS, HEAD_DIM)
        out = jnp.concatenate([rows[:, t_new * KV_ROWS:], new], axis=1)
        return out.reshape(kv_shape(n_seq, rows.shape[1] // KV_ROWS, N_KV))

    for li in range(depth):
        lk = li // 2
        n_mem = mem_prompt.shape[1]
        mkv_p = _matmul(mem_prompt.reshape(-1, d), w_mem_kv[li].astype(BF16), batch * n_mem, 2 * QM_W)
        mem_p.append(mkv_p.reshape(batch, n_mem, 2, N_MEM_HEADS, HEAD_DIM))
        mkv_p = mkv_p.reshape(1, batch, n_mem, 2 * QM_W)

        if li % 2 == 0:
            w_in = w_in_a[lk]
            g0 = Q_W + 3 * KV_W
            gate_w = w_in[:, g0:g0 + 3 * N_HEADS].reshape(d, N_KV, 3 * HPG)
            gate_w = jnp.pad(gate_w, ((0, 0), (0, 0), (0, LANES - 3 * HPG))).reshape(d, N_KV * LANES)
            w_r = jnp.concatenate([w_in[:, :g0], w_in[:, g0 + 3 * N_HEADS:], gate_w], axis=1).astype(BF16)
            bg = jnp.pad(b_gate_a[lk].reshape(N_KV, 3 * HPG), ((0, 0), (0, LANES - 3 * HPG))).reshape(1, N_KV * LANES)
            h = _matmul(x, w_r, TM_PROJ, A_WIDTH // 6)

            w_cmp = w_cmp_a[lk].astype(BF16)
            wab = jnp.concatenate([w_cmp[:, :CMP_STRIDE], w_cmp[:, CMP_STRIDE:]], axis=-1)
            wab = wab.reshape(2, CMP_STRIDE // 2, 2 * HEAD_DIM, 2 * HEAD_DIM)
            cmp_bias = _cmp_bias(pe_cmp_a[lk], w_cmp)
            hp = h[:n_prompt].reshape(batch, seq, -1)
            hs = h[n_prompt:].reshape(n_seq, t_new, -1)
            a_cmp_p.append(hp[..., A_KC:A_KC + KV_W].reshape(kv_shape(batch, seq, N_KV)))
            kvc_p = _compress_prompt(a_cmp_p[-1].reshape(-1, HEAD_DIM), wab, cmp_bias, batch, seq)
            n_blk_p = seq // SEL_BLK
            ov_p = _overlap_table(seq // CMP_STRIDE, n_blk_p, _round_up(n_blk_p, LANES))
            ex_p = _expand_table(_round_up(n_blk_p, LANES), seq)
            mix_p = _nsa_prompt(h, kvc_p, slopes, bg, ov_p, ex_p, batch, seq, tq)

            n_blk_s = -(-(past + t_new) // SEL_BLK)
            nbl = _round_up(n_blk_s, LANES)
            ov_s = _overlap_table(past // CMP_STRIDE, n_blk_s, nbl)
            ex_s = _expand_table(nbl, past)
            oc_s, sel_s = _nsa_sample_cmp(h, n_prompt, cmp_pages, lk, page_table, slopes, wab, cmp_bias, ov_s,
                                          n_seq, t_new, past)
            mix_s = _nsa_sample_sel(h, n_prompt, sel_pages, lk, page_table, slopes, sel_s, ex_s, oc_s, bg, win_rows,
                                    n_seq, t_new, past)

            a_cmp_s.append(hs[..., A_KC:A_KC + KV_W].reshape(kv_shape(n_seq, t_new, N_KV)))
            a_sel_p.append(hp[..., A_KS:A_KS + KV_W].reshape(kv_shape(batch, seq, N_KV)))
            a_sel_s.append(hs[..., A_KS:A_KS + KV_W].reshape(kv_shape(n_seq, t_new, N_KV)))
            wn = min(WIN_A, seq)
            a_win_p.append(hp[:, seq - wn:, A_KW:A_KW + KV_W].reshape(kv_shape(batch, wn, N_KV)))
            a_win_s.append(shifted_rows(win_rows[lk], hs[..., A_KW:A_KW + KV_W]))
            qm_col = A_QM
        else:
            h = _matmul(x, w_in_b[lk].astype(BF16), TM_PROJ, B_WIDTH // 4)
            mix_p = _swa_prompt(h, slopes, sinks_b[lk], batch, seq, tq)
            mix_s = _swa_sample(h, n_prompt, buf_rows, lk, slopes, sinks_b[lk], n_seq, t_new, past)
            hp = h[:n_prompt].reshape(batch, seq, -1)
            hs = h[n_prompt:].reshape(n_seq, t_new, -1)
            wn = min(WIN_B, seq)
            b_kv_p.append(hp[:, seq - wn:, B_KV:B_KV + KV_W].reshape(kv_shape(batch, wn, N_KV)))
            b_kv_s.append(shifted_rows(buf_rows[lk], hs[..., B_KV:B_KV + KV_W]))
            qm_col = B_QM

        mem_o_p = _mem_attend(h, qm_col, 0, mkv_p, 0, tq, seq // tq)
        mem_o_s = _mem_attend(h, qm_col, n_prompt, mem_rows, li, t_new, 1, math.gcd(n_seq, MEM_SEQS_PER_STEP))
        mix = jnp.concatenate([mix_p, mix_s], axis=0)
        mem_o = jnp.concatenate([mem_o_p, mem_o_s], axis=0)
        w_o = w_out[li].astype(BF16)
        x = _outproj_ln(mix, mem_o, w_o[:Q_W], w_o[Q_W:], b_out[li].reshape(1, d), x,
                        ln_g[li, 0].reshape(1, d), ln_b[li, 0].reshape(1, d), alpha, TM_OUT)

        sk = peer_subkeys[li].reshape(2 * PEER_HEADS, PEER_KEYS, PEER_HALF).astype(BF16)
        route = _peer_route(x, peer_wq[li].T.astype(BF16), sk, TB_PEER)
        x = _peer_experts(x, route, peer_ub, peer_vt, li, ln_g[li, 1].reshape(1, d), ln_b[li, 1].reshape(1, d),
                          alpha, TB_PEER, EC_PEER)

    return (x[:n_prompt].reshape(batch, seq, d), x[n_prompt:].reshape(n_seq, t_new, d),
            jnp.stack(a_cmp_p), jnp.stack(a_cmp_s), jnp.stack(a_sel_p), jnp.stack(a_sel_s),
            jnp.stack(a_win_p), jnp.stack(a_win_s), jnp.stack(b_kv_p), jnp.stack(b_kv_s), jnp.stack(mem_p))
```

```python
import functools
import math

import jax
import jax.numpy as jnp
from jax import lax
from jax.experimental import pallas as pl
from jax.experimental.pallas import tpu as pltpu

F32 = jnp.float32
BF16 = jnp.bfloat16

HEAD_DIM = 128
N_MEM_HEADS = 4
N_KV = 2
HPG = 6
N_HEADS = N_KV * HPG
Q_W = N_HEADS * HEAD_DIM
KV_W = 2 * N_KV * HEAD_DIM
KV_ROWS = 2 * N_KV
QM_W = N_MEM_HEADS * HEAD_DIM
CMP_STRIDE = 16
CMP_LEN = 32
SEL_BLK = 64
N_SEL = 16
WIN_A = 512
WIN_B = 128
PAGE_SIZE = 128
PEER_HEADS = 8
PEER_KEYS = 128
PEER_TOPK = 16
PEER_HALF = 128
LN_EPS = 1e-5
NEG_INF = -1e30
SEL_FORCE = 1e4
SCALE = HEAD_DIM ** -0.5
SQRT_HALF = math.sqrt(0.5)
LANES = 128
MXU_N = 256
ACC_ROWS = 256
GATE_ROWS = 16

TM_PROJ = 1024
TM_OUT = 512
TQ_ATTN = 256
TB_PEER = 512
EC_PEER = 512
PAGES_PER_STEP = 32
MEM_SEQS_PER_STEP = 8

A_KC, A_KS, A_KW, A_QM, A_GATE = Q_W, Q_W + KV_W, Q_W + 2 * KV_W, Q_W + 3 * KV_W, Q_W + 3 * KV_W + QM_W
A_WIDTH = A_GATE + N_KV * LANES
B_KV, B_QM = Q_W, Q_W + KV_W
B_WIDTH = B_QM + QM_W


def _params(sem, vmem_mb, flags=None):
    return pltpu.CompilerParams(dimension_semantics=sem, vmem_limit_bytes=vmem_mb << 20, flags=flags)


def _dot(a, b):
    return jnp.dot(a, b, preferred_element_type=F32)


def _dot_nt(a, b):
    return lax.dot_general(a, b, (((1,), (1,)), ((), ())), preferred_element_type=F32)


def _mm_kernel(x_ref, w_ref, o_ref):
    o_ref[...] = _dot(x_ref[...].astype(BF16), w_ref[...])


def _matmul(x, w, tm, tn):
    m, k = x.shape
    n = w.shape[1]
    return pl.pallas_call(
        _mm_kernel,
        grid=(m // tm, n // tn),
        in_specs=[pl.BlockSpec((tm, k), lambda i, j: (i, 0)), pl.BlockSpec((k, tn), lambda i, j: (0, j))],
        out_specs=pl.BlockSpec((tm, tn), lambda i, j: (i, j)),
        out_shape=jax.ShapeDtypeStruct((m, n), F32),
        compiler_params=_params(("parallel", "parallel"), 40),
        name="proj",
    )(x, w)


def _layer_norm(z, g, b):
    mu = jnp.mean(z, axis=-1, keepdims=True)
    zc = z - mu
    var = jnp.mean(zc * zc, axis=-1, keepdims=True)
    return zc * lax.rsqrt(var + LN_EPS) * g + b


def _outproj_kernel(alpha, mix_ref, mem_ref, w1_ref, w2_ref, bias_ref, x_ref, g_ref, b_ref, o_ref):
    y = _dot(mix_ref[...].astype(BF16), w1_ref[...]) + _dot(mem_ref[...].astype(BF16), w2_ref[...]) + bias_ref[...]
    o_ref[...] = _layer_norm(alpha * x_ref[...] + y, g_ref[...], b_ref[...])


def _outproj_ln(mix, mem, w1, w2, bias, x, g, b, alpha, tm):
    m, d = x.shape
    row = lambda i: (i, 0)
    fixed = lambda i: (0, 0)
    return pl.pallas_call(
        functools.partial(_outproj_kernel, alpha),
        grid=(m // tm,),
        in_specs=[pl.BlockSpec((tm, Q_W), row), pl.BlockSpec((tm, QM_W), row),
                  pl.BlockSpec((Q_W, d), fixed), pl.BlockSpec((QM_W, d), fixed), pl.BlockSpec((1, d), fixed),
                  pl.BlockSpec((tm, d), row), pl.BlockSpec((1, d), fixed), pl.BlockSpec((1, d), fixed)],
        out_specs=pl.BlockSpec((tm, d), row),
        out_shape=jax.ShapeDtypeStruct((m, d), F32),
        compiler_params=_params(("parallel",), 48),
        name="outproj_ln",
    )(mix, mem, w1, w2, bias, x, g, b)


def _flash_init(rows):
    return (jnp.full((rows, 1), NEG_INF, F32), jnp.zeros((rows, 1), F32), jnp.zeros((rows, HEAD_DIM), F32))


def _flash_tile(carry, q, k, v, slope_col, qpos_col, kpos_row, window=None, extra=None, causal=True, guard=True):
    m, l, acc = carry
    dist = qpos_col - kpos_row
    s = _dot_nt(q, k) * SCALE - slope_col * dist.astype(F32)
    mask = None
    for cond in ((dist >= 0) if causal else None, (dist < window) if window is not None else None, extra):
        if cond is not None:
            mask = cond if mask is None else jnp.logical_and(mask, cond)
    if mask is not None:
        s = jnp.where(mask, s, NEG_INF)
    m_new = jnp.maximum(m, jnp.max(s, axis=1, keepdims=True))
    e = jnp.exp(s - m_new)
    if mask is not None and guard:
        e = jnp.where(mask, e, 0.0)
    corr = jnp.exp(m - m_new)
    l = l * corr + jnp.sum(e, axis=1, keepdims=True)
    acc = acc * corr + _dot(e.astype(BF16), v)
    return m_new, l, acc


def _window_tiles(tile_fn, carry, qi, tq, window):
    for delta in range((window + tq - 2) // tq, -1, -1):
        at_edge = delta * tq + tq - 1 >= window
        masks = dict(window=window if at_edge else None, causal=delta == 0, guard=at_edge and delta > 0)
        if delta == 0:
            carry = tile_fn(qi, carry, **masks)
        else:
            body = lambda _, c, delta=delta, masks=masks: tile_fn(qi - delta, c, **masks)
            carry = lax.fori_loop(0, (qi >= delta).astype(jnp.int32), body, carry)
    return carry


def _flash_out(carry):
    _, l, acc = carry
    return acc / jnp.maximum(l, 1e-30)


def _flash_out_sink(carry, sink_col):
    m, l, acc = carry
    m_f = jnp.maximum(m, sink_col)
    corr = jnp.exp(m - m_f)
    return acc * corr / (l * corr + jnp.exp(sink_col - m_f))


def _stack_heads(q_ref, nq):
    return jnp.concatenate([q_ref[:, hh * HEAD_DIM:(hh + 1) * HEAD_DIM] for hh in range(HPG)], axis=0).astype(BF16)


def _head_cols(ref, base, nq):
    return jnp.concatenate([jnp.full((nq, 1), ref[base + hh], F32) for hh in range(HPG)], axis=0)


def _stacked_pos(q0, nq):
    pos = q0 + lax.broadcasted_iota(jnp.int32, (nq, 1), 0)
    return jnp.concatenate([pos] * HPG, axis=0)


def _tile_rows(x):
    return jnp.concatenate([x] * HPG, axis=0)


def _top_blocks(score, lane):
    sel = jnp.zeros_like(score)
    work = score
    for _ in range(N_SEL):
        hit = lane == jnp.argmax(work, axis=1, keepdims=True).astype(jnp.int32)
        sel = jnp.where(hit, 1.0, sel)
        work = jnp.where(hit, -jnp.inf, work)
    return sel


def _cmp_branch(q_all, kc, vc, slope_col, qpos_col, ov_ref, nq, n_blk):
    ncp = kc.shape[0]
    end_row = lax.broadcasted_iota(jnp.int32, (1, ncp), 1) * CMP_STRIDE + (CMP_LEN - 1)
    dist = qpos_col - end_row
    mask = dist >= 0
    s = jnp.where(mask, _dot_nt(q_all, kc) * SCALE - slope_col * dist.astype(F32), NEG_INF)
    e = jnp.where(mask, jnp.exp(s - jnp.max(s, axis=1, keepdims=True)), 0.0)
    p = e / jnp.maximum(jnp.sum(e, axis=1, keepdims=True), 1e-30)
    o_c = _dot(p.astype(BF16), vc)
    imp_c = p[0:nq]
    for hh in range(1, HPG):
        imp_c = imp_c + p[hh * nq:(hh + 1) * nq]
    hi = imp_c.astype(BF16)
    lo = (imp_c - hi.astype(F32)).astype(BF16)
    imp = _dot(hi, ov_ref[...]) + _dot(lo, ov_ref[...])
    nl = imp.shape[1]
    lane_i = lax.broadcasted_iota(jnp.int32, (nq, nl), 1)
    cur = qpos_col[0:nq] // SEL_BLK
    forced = jnp.logical_or(lane_i == 0, jnp.logical_or(lane_i == cur, lane_i == cur - 1))
    score = jnp.where(forced, SEL_FORCE, jnp.where(lane_i <= cur, imp, -SEL_FORCE))
    score = jnp.where(lane_i < n_blk, score, -jnp.inf)
    return o_c, _top_blocks(score, lane_i)


def _gated_sum(gate_ref, bg_ref, branches, o_ref, nq):
    gates = jax.nn.sigmoid(gate_ref[...] + bg_ref[...])
    for hh in range(HPG):
        rows = slice(hh * nq, (hh + 1) * nq)
        o = gates[:, 3 * hh:3 * hh + 1] * branches[0][rows]
        for j in (1, 2):
            o = o + gates[:, 3 * hh + j:3 * hh + j + 1] * branches[j][rows]
        o_ref[:, hh * HEAD_DIM:(hh + 1) * HEAD_DIM] = o


def _chunk_ab(load_rows, wab_ref, c):
    acc = None
    for p in range(CMP_STRIDE // 2):
        x = jnp.concatenate([load_rows(2 * p), load_rows(2 * p + 1)], axis=1).astype(BF16)
        t = _dot(x, wab_ref[c, p])
        acc = t if acc is None else acc + t
    return acc


def _cmp_bias_kernel(pe_ref, w_ref, o_ref):
    for c in range(2):
        acc = jnp.zeros((8, HEAD_DIM), F32)
        for r in range(CMP_LEN):
            acc = acc + _dot(jnp.broadcast_to(pe_ref[c, r:r + 1, :], (8, HEAD_DIM)).astype(BF16), w_ref[c, r])
        o_ref[c] = acc


def _cmp_bias(pe, w):
    return pl.pallas_call(
        _cmp_bias_kernel,
        out_shape=jax.ShapeDtypeStruct((2, 8, HEAD_DIM), F32),
        name="cmp_bias",
    )(pe, w)


def _blocks_from_ab(ab, bias):
    n = ab.shape[0]
    return ab[:, :HEAD_DIM] + pltpu.roll(ab[:, HEAD_DIM:], n - 1, 0) + bias


def _compress_kernel(nch, kv_ref, wab_ref, bias_ref, o_ref):
    for c in range(2):
        for g in range(N_KV):
            cg = c * N_KV + g
            ab = _chunk_ab(lambda r: kv_ref[pl.ds(r * KV_ROWS + cg, nch, stride=CMP_STRIDE * KV_ROWS), :], wab_ref, c)
            o_ref[c, g] = _blocks_from_ab(ab, bias_ref[c, 0:1, :])


def _compress_prompt(kv, wab, bias, batch, seq):
    nch = seq // CMP_STRIDE
    return pl.pallas_call(
        functools.partial(_compress_kernel, nch),
        grid=(batch,),
        in_specs=[pl.BlockSpec((seq * KV_ROWS, HEAD_DIM), lambda b: (b, 0)),
                  pl.BlockSpec(wab.shape, lambda b: (0, 0, 0, 0)), pl.BlockSpec(bias.shape, lambda b: (0, 0, 0))],
        out_specs=pl.BlockSpec((None, 2, N_KV, nch, HEAD_DIM), lambda b: (b, 0, 0, 0, 0)),
        out_shape=jax.ShapeDtypeStruct((batch, 2, N_KV, nch, HEAD_DIM), F32),
        compiler_params=_params(("parallel",), 40),
        name="compress_prompt",
    )(kv, wab, bias)


def _nsa_prompt_kernel(tq, seq, slopes_ref, q_ref, gate_ref, bg_ref, kvc_ref, ks_ref, vs_ref, kw_ref, vw_ref,
                       ov_ref, e_ref, o_ref, selx_ref):
    g = pl.program_id(1)
    qi = pl.program_id(2)
    q0 = qi * tq
    q_all = _stack_heads(q_ref, tq)
    slope_col = _head_cols(slopes_ref, g * HPG, tq)
    qpos_col = _stacked_pos(q0, tq)

    o_c, sel = _cmp_branch(q_all, kvc_ref[0].astype(BF16), kvc_ref[1].astype(BF16), slope_col, qpos_col, ov_ref,
                           tq, seq // SEL_BLK)
    selx_ref[...] = _dot(sel.astype(BF16), e_ref[...])

    def key_tile(k_ref, v_ref, kj):
        k0 = pl.multiple_of(kj * tq, tq)
        kpos_row = k0 + lax.broadcasted_iota(jnp.int32, (1, tq), 1)
        return k0, k_ref[pl.ds(k0, tq), :].astype(BF16), v_ref[pl.ds(k0, tq), :].astype(BF16), kpos_row

    def sel_tile(kj, carry, diagonal):
        k0, k, v, kpos_row = key_tile(ks_ref, vs_ref, kj)
        extra = _tile_rows(selx_ref[:, pl.ds(k0, tq)]) > 0.5
        return _flash_tile(carry, q_all, k, v, slope_col, qpos_col, kpos_row, extra=extra, causal=diagonal,
                           guard=False)

    def win_tile(kj, carry, **masks):
        _, k, v, kpos_row = key_tile(kw_ref, vw_ref, kj)
        return _flash_tile(carry, q_all, k, v, slope_col, qpos_col, kpos_row, **masks)

    rows = HPG * tq
    carry = lax.fori_loop(0, qi, functools.partial(sel_tile, diagonal=False), _flash_init(rows))
    o_s = _flash_out(sel_tile(qi, carry, True))
    o_w = _flash_out(_window_tiles(win_tile, _flash_init(rows), qi, tq, WIN_A))
    _gated_sum(gate_ref, bg_ref, (o_c, o_s, o_w), o_ref, tq)


def _nsa_prompt(h, kvc, slopes, bg, ov, expand, batch, seq, tq):
    nq = seq // tq
    nch = kvc.shape[3]
    kv_spec = lambda col: pl.BlockSpec((seq, HEAD_DIM), lambda b, g, i, col=col: (b, col // HEAD_DIM + g))
    return pl.pallas_call(
        functools.partial(_nsa_prompt_kernel, tq, seq),
        grid=(batch, N_KV, nq),
        in_specs=[pl.BlockSpec(memory_space=pltpu.SMEM),
                  pl.BlockSpec((tq, HPG * HEAD_DIM), lambda b, g, i: (b * nq + i, g)),
                  pl.BlockSpec((tq, LANES), lambda b, g, i: (b * nq + i, A_GATE // LANES + g)),
                  pl.BlockSpec((1, LANES), lambda b, g, i: (0, g)),
                  pl.BlockSpec((None, 2, None, nch, HEAD_DIM), lambda b, g, i: (b, 0, g, 0, 0)),
                  kv_spec(A_KS), kv_spec(A_KS + N_KV * HEAD_DIM), kv_spec(A_KW), kv_spec(A_KW + N_KV * HEAD_DIM),
                  pl.BlockSpec(ov.shape, lambda b, g, i: (0, 0)),
                  pl.BlockSpec(expand.shape, lambda b, g, i: (0, 0))],
        out_specs=pl.BlockSpec((tq, HPG * HEAD_DIM), lambda b, g, i: (b * nq + i, g)),
        out_shape=jax.ShapeDtypeStruct((batch * seq, Q_W), F32),
        scratch_shapes=[pltpu.VMEM((tq, seq), F32)],
        compiler_params=_params(("parallel", "parallel", "arbitrary"), 48),
        name="nsa_prompt",
    )(slopes, h, h, bg, kvc, h, h, h, h, ov, expand)


def _swa_prompt_kernel(tq, slopes_ref, sinks_ref, q_ref, k_ref, v_ref, o_ref):
    g = pl.program_id(1)
    qi = pl.program_id(2)
    q_all = _stack_heads(q_ref, tq)
    slope_col = _head_cols(slopes_ref, g * HPG, tq)
    sink_col = _head_cols(sinks_ref, g * HPG, tq)
    qpos_col = _stacked_pos(qi * tq, tq)

    def tile(kj, carry, **masks):
        k0 = pl.multiple_of(kj * tq, tq)
        kpos_row = k0 + lax.broadcasted_iota(jnp.int32, (1, tq), 1)
        k = k_ref[pl.ds(k0, tq), :].astype(BF16)
        v = v_ref[pl.ds(k0, tq), :].astype(BF16)
        return _flash_tile(carry, q_all, k, v, slope_col, qpos_col, kpos_row, **masks)

    o = _flash_out_sink(_window_tiles(tile, _flash_init(HPG * tq), qi, tq, WIN_B), sink_col)
    for hh in range(HPG):
        o_ref[:, hh * HEAD_DIM:(hh + 1) * HEAD_DIM] = o[hh * tq:(hh + 1) * tq]


def _swa_prompt(h, slopes, sinks, batch, seq, tq):
    nq = seq // tq
    kv_spec = lambda col: pl.BlockSpec((seq, HEAD_DIM), lambda b, g, i, col=col: (b, col // HEAD_DIM + g))
    smem = pl.BlockSpec(memory_space=pltpu.SMEM)
    return pl.pallas_call(
        functools.partial(_swa_prompt_kernel, tq),
        grid=(batch, N_KV, nq),
        in_specs=[smem, smem,
                  pl.BlockSpec((tq, HPG * HEAD_DIM), lambda b, g, i: (b * nq + i, g)),
                  kv_spec(B_KV), kv_spec(B_KV + N_KV * HEAD_DIM)],
        out_specs=pl.BlockSpec((tq, HPG * HEAD_DIM), lambda b, g, i: (b * nq + i, g)),
        out_shape=jax.ShapeDtypeStruct((batch * seq, Q_W), F32),
        compiler_params=_params(("parallel", "parallel", "arbitrary"), 40),
        name="swa_prompt",
    )(slopes, sinks, h, h, h)


def _mem_kernel(row_layout, q_ref, kv_ref, o_ref):
    n_sub = kv_ref.shape[0]
    tq = q_ref.shape[0] // n_sub
    for sub in range(n_sub):
        rows = slice(sub * tq, (sub + 1) * tq)
        for hd in range(N_MEM_HEADS):
            cols = slice(hd * HEAD_DIM, (hd + 1) * HEAD_DIM)
            if row_layout:
                n_mem = kv_ref.shape[1] // (2 * N_MEM_HEADS)
                k = kv_ref[sub, pl.ds(hd, n_mem, stride=2 * N_MEM_HEADS), :].astype(BF16)
                v = kv_ref[sub, pl.ds(N_MEM_HEADS + hd, n_mem, stride=2 * N_MEM_HEADS), :].astype(BF16)
            else:
                k = kv_ref[sub, :, hd * HEAD_DIM:(hd + 1) * HEAD_DIM].astype(BF16)
                v = kv_ref[sub, :, QM_W + hd * HEAD_DIM:QM_W + (hd + 1) * HEAD_DIM].astype(BF16)
            s = _dot_nt(q_ref[rows, cols].astype(BF16), k) * SCALE
            e = jnp.exp(s - jnp.max(s, axis=1, keepdims=True))
            p = e / jnp.sum(e, axis=1, keepdims=True)
            o_ref[rows, cols] = _dot(p.astype(BF16), v)


def _mem_attend(h, qm_col, row0, mkv, layer, tq, blocks_per_batch, n_sub=1):
    _, n_batch, kv_rows, kv_cols = mkv.shape
    assert n_sub == 1 or blocks_per_batch == 1
    nblk = n_batch * blocks_per_batch // n_sub
    rows = tq * n_sub
    return pl.pallas_call(
        functools.partial(_mem_kernel, kv_cols == HEAD_DIM),
        grid=(nblk,),
        in_specs=[pl.BlockSpec((rows, QM_W), lambda i: (row0 // rows + i, qm_col // QM_W)),
                  pl.BlockSpec((None, n_sub, kv_rows, kv_cols), lambda i: (layer, i // blocks_per_batch, 0, 0))],
        out_specs=pl.BlockSpec((rows, QM_W), lambda i: (i, 0)),
        out_shape=jax.ShapeDtypeStruct((nblk * rows, QM_W), F32),
        compiler_params=_params(("parallel",), 32),
        name="mem_attend",
    )(h, mkv)


def _nsa_sample_cmp_kernel(t_new, past, *refs):
    npg = PAGES_PER_STEP
    _, slopes_ref = refs[0], refs[1]
    page_refs = refs[2:2 + npg]
    q_ref, wab_ref, bias_ref, ov_ref, oc_ref, sel_ref, ab_ref = refs[2 + npg:]
    j = pl.program_id(1)
    rows_per_step = npg * PAGE_SIZE // CMP_STRIDE
    pages = [jnp.swapaxes(p[...], 0, 1) for p in page_refs]
    for c in range(2):
        for g in range(N_KV):
            cg = c * N_KV + g
            ab = _chunk_ab(lambda r: jnp.concatenate([x[r * KV_ROWS + cg] for x in pages], axis=0), wab_ref, c)
            ab_ref[cg, pl.ds(pl.multiple_of(j * rows_per_step, rows_per_step), rows_per_step), :] = ab

    @pl.when(j == pl.num_programs(1) - 1)
    def _():
        bias = [bias_ref[c, 0:1, :] for c in range(2)]
        n_blk = -(-(past + t_new) // SEL_BLK)
        for g in range(N_KV):
            kc = _blocks_from_ab(ab_ref[g], bias[0]).astype(BF16)
            vc = _blocks_from_ab(ab_ref[N_KV + g], bias[1]).astype(BF16)
            q_all = _stack_heads(q_ref.at[:, g * HPG * HEAD_DIM:(g + 1) * HPG * HEAD_DIM], t_new)
            slope_col = _head_cols(slopes_ref, g * HPG, t_new)
            qpos_col = _stacked_pos(past, t_new)
            o_c, sel = _cmp_branch(q_all, kc, vc, slope_col, qpos_col, ov_ref, t_new, n_blk)
            for hh in range(HPG):
                oc_ref[:, (g * HPG + hh) * HEAD_DIM:(g * HPG + hh + 1) * HEAD_DIM] = o_c[hh * t_new:(hh + 1) * t_new]
            sel_ref[g] = sel


def _nsa_sample_cmp(h, row0, cache, lk, page_table, slopes, wab, bias, ov, n_seq, t_new, past):
    n_pages = past // PAGE_SIZE
    npg = PAGES_PER_STEP
    nch = past // CMP_STRIDE
    page_spec = lambda k: pl.BlockSpec((None, None, PAGE_SIZE // CMP_STRIDE, CMP_STRIDE * KV_ROWS, HEAD_DIM),
                                       lambda b, j, pt, k=k: (lk, pt[b, j * npg + k], 0, 0, 0))
    fixed = lambda nd: (lambda b, j, pt: (0,) * nd)
    grid_spec = pltpu.PrefetchScalarGridSpec(
        num_scalar_prefetch=1,
        grid=(n_seq, n_pages // npg),
        in_specs=[pl.BlockSpec(memory_space=pltpu.SMEM)] + [page_spec(k) for k in range(npg)] + [
            pl.BlockSpec((t_new, Q_W), lambda b, j, pt: (row0 // t_new + b, 0)),
            pl.BlockSpec(wab.shape, fixed(4)), pl.BlockSpec(bias.shape, fixed(3)), pl.BlockSpec(ov.shape, fixed(2))],
        out_specs=[pl.BlockSpec((t_new, Q_W), lambda b, j, pt: (b, 0)),
                   pl.BlockSpec((None, N_KV, t_new, ov.shape[1]), lambda b, j, pt: (b, 0, 0, 0))],
        scratch_shapes=[pltpu.VMEM((2 * N_KV, nch, 2 * HEAD_DIM), F32)],
    )
    return pl.pallas_call(
        functools.partial(_nsa_sample_cmp_kernel, t_new, past),
        grid_spec=grid_spec,
        out_shape=[jax.ShapeDtypeStruct((n_seq * t_new, Q_W), F32),
                   jax.ShapeDtypeStruct((n_seq, N_KV, t_new, ov.shape[1]), F32)],
        compiler_params=_params(("parallel", "arbitrary"), 48),
        name="nsa_sample_cmp",
    )(page_table, slopes, *([cache] * npg), h, wab, bias, ov)


def _kv_rows(ref, c, g):
    return ref[pl.ds(c * N_KV + g, ref.shape[0] // KV_ROWS, stride=KV_ROWS), :].astype(BF16)


def _pad_keys(new_ref, col, t_new):
    rows = new_ref[:, col:col + HEAD_DIM]
    return jnp.concatenate([rows, jnp.zeros((LANES - t_new, HEAD_DIM), F32)], axis=0).astype(BF16)


def _nsa_sample_sel_kernel(t_new, past, *refs):
    npg = PAGES_PER_STEP
    slopes_ref = refs[1]
    page_refs = refs[2:2 + npg]
    (q_ref, sel_ref, e_ref, oc_ref, ks_new_ref, kw_new_ref, gate_ref, bg_ref, win_ref,
     o_ref, m_ref, l_ref, acc_ref) = refs[2 + npg:]
    j = pl.program_id(1)
    rows = HPG * t_new
    keys = npg * PAGE_SIZE
    qpos_col = _stacked_pos(past, t_new)

    @pl.when(j == 0)
    def _():
        m_ref[...] = jnp.full(m_ref.shape, NEG_INF, F32)
        l_ref[...] = jnp.zeros(l_ref.shape, F32)
        acc_ref[...] = jnp.zeros(acc_ref.shape, F32)

    kpos_row = j * keys + lax.broadcasted_iota(jnp.int32, (1, keys), 1)
    sel_keys = _dot(jnp.concatenate([sel_ref[g] for g in range(N_KV)], axis=0).astype(BF16), e_ref[...])
    for g in range(N_KV):
        q_all = _stack_heads(q_ref.at[:, g * HPG * HEAD_DIM:(g + 1) * HPG * HEAD_DIM], t_new)
        slope_col = _head_cols(slopes_ref, g * HPG, t_new)
        k = jnp.concatenate([_kv_rows(p, 0, g) for p in page_refs], axis=0)
        v = jnp.concatenate([_kv_rows(p, 1, g) for p in page_refs], axis=0)
        extra = _tile_rows(sel_keys[g * t_new:(g + 1) * t_new]) > 0.5
        carry = _flash_tile((m_ref[g], l_ref[g], acc_ref[g]), q_all, k, v, slope_col, qpos_col, kpos_row, extra=extra)
        m_ref[g], l_ref[g], acc_ref[g] = carry

    @pl.when(j == pl.num_programs(1) - 1)
    def _():
        new_pos = past + lax.broadcasted_iota(jnp.int32, (1, LANES), 1)
        wb = win_ref.shape[0] // KV_ROWS
        win_pos = past - wb + lax.broadcasted_iota(jnp.int32, (1, wb), 1)
        for g in range(N_KV):
            q_all = _stack_heads(q_ref.at[:, g * HPG * HEAD_DIM:(g + 1) * HPG * HEAD_DIM], t_new)
            slope_col = _head_cols(slopes_ref, g * HPG, t_new)
            sel_new = _tile_rows(sel_ref[g][:, past // SEL_BLK:past // SEL_BLK + 1]) > 0.5
            carry = _flash_tile((m_ref[g], l_ref[g], acc_ref[g]), q_all,
                                _pad_keys(ks_new_ref, g * HEAD_DIM, t_new),
                                _pad_keys(ks_new_ref, (N_KV + g) * HEAD_DIM, t_new),
                                slope_col, qpos_col, new_pos, extra=sel_new)
            o_s = _flash_out(carry)
            carry = _flash_tile(_flash_init(rows), q_all, _kv_rows(win_ref, 0, g), _kv_rows(win_ref, 1, g),
                                slope_col, qpos_col, win_pos, window=WIN_A)
            carry = _flash_tile(carry, q_all, _pad_keys(kw_new_ref, g * HEAD_DIM, t_new),
                                _pad_keys(kw_new_ref, (N_KV + g) * HEAD_DIM, t_new),
                                slope_col, qpos_col, new_pos, window=WIN_A)
            o_w = _flash_out(carry)
            o_c = jnp.concatenate([oc_ref[:, (g * HPG + hh) * HEAD_DIM:(g * HPG + hh + 1) * HEAD_DIM]
                                   for hh in range(HPG)], axis=0)
            _gated_sum(gate_ref.at[:, g * LANES:(g + 1) * LANES], bg_ref.at[:, g * LANES:(g + 1) * LANES],
                       (o_c, o_s, o_w), o_ref.at[:, g * HPG * HEAD_DIM:(g + 1) * HPG * HEAD_DIM], t_new)


def _nsa_sample_sel(h, row0, cache, lk, page_table, slopes, sel, expand, oc, bg, win, n_seq, t_new, past):
    n_pages = past // PAGE_SIZE
    npg = PAGES_PER_STEP
    nbl = sel.shape[3]
    win_rows = win.shape[2]
    rows = HPG * t_new
    page_spec = lambda k: pl.BlockSpec((None, None, PAGE_SIZE * KV_ROWS, HEAD_DIM),
                                       lambda b, j, pt, k=k: (lk, pt[b, j * npg + k], 0, 0))
    hrow = lambda width, col: pl.BlockSpec((t_new, width), lambda b, j, pt: (row0 // t_new + b, col // width))
    grid_spec = pltpu.PrefetchScalarGridSpec(
        num_scalar_prefetch=1,
        grid=(n_seq, n_pages // npg),
        in_specs=[pl.BlockSpec(memory_space=pltpu.SMEM)] + [page_spec(k) for k in range(npg)] + [
            hrow(Q_W, 0),
            pl.BlockSpec((None, N_KV, t_new, nbl), lambda b, j, pt: (b, 0, 0, 0)),
            pl.BlockSpec((nbl, npg * PAGE_SIZE), lambda b, j, pt: (0, j)),
            pl.BlockSpec((t_new, Q_W), lambda b, j, pt: (b, 0)),
            hrow(KV_W, A_KS), hrow(KV_W, A_KW), hrow(N_KV * LANES, A_GATE),
            pl.BlockSpec((1, N_KV * LANES), lambda b, j, pt: (0, 0)),
            pl.BlockSpec((None, None, win_rows, HEAD_DIM), lambda b, j, pt: (lk, b, 0, 0))],
        out_specs=pl.BlockSpec((t_new, Q_W), lambda b, j, pt: (b, 0)),
        scratch_shapes=[pltpu.VMEM((N_KV, rows, 1), F32), pltpu.VMEM((N_KV, rows, 1), F32),
                        pltpu.VMEM((N_KV, rows, HEAD_DIM), F32)],
    )
    return pl.pallas_call(
        functools.partial(_nsa_sample_sel_kernel, t_new, past),
        grid_spec=grid_spec,
        out_shape=jax.ShapeDtypeStruct((n_seq * t_new, Q_W), F32),
        compiler_params=_params(("parallel", "arbitrary"), 48),
        name="nsa_sample_sel",
    )(page_table, slopes, *([cache] * npg), h, sel, expand, oc, h, h, h, bg, win)


def _swa_sample_kernel(t_new, past, slopes_ref, sinks_ref, q_ref, new_ref, buf_ref, o_ref):
    wb = buf_ref.shape[0] // KV_ROWS
    qpos_col = _stacked_pos(past, t_new)
    buf_pos = past - wb + lax.broadcasted_iota(jnp.int32, (1, wb), 1)
    new_pos = past + lax.broadcasted_iota(jnp.int32, (1, LANES), 1)
    for g in range(N_KV):
        q_all = _stack_heads(q_ref.at[:, g * HPG * HEAD_DIM:(g + 1) * HPG * HEAD_DIM], t_new)
        slope_col = _head_cols(slopes_ref, g * HPG, t_new)
        sink_col = _head_cols(sinks_ref, g * HPG, t_new)
        carry = _flash_tile(_flash_init(HPG * t_new), q_all, _kv_rows(buf_ref, 0, g), _kv_rows(buf_ref, 1, g),
                            slope_col, qpos_col, buf_pos, window=WIN_B)
        carry = _flash_tile(carry, q_all, _pad_keys(new_ref, g * HEAD_DIM, t_new),
                            _pad_keys(new_ref, (N_KV + g) * HEAD_DIM, t_new),
                            slope_col, qpos_col, new_pos, window=WIN_B)
        o = _flash_out_sink(carry, sink_col)
        for hh in range(HPG):
            o_ref[:, (g * HPG + hh) * HEAD_DIM:(g * HPG + hh + 1) * HEAD_DIM] = o[hh * t_new:(hh + 1) * t_new]


def _swa_sample(h, row0, buf, lk, slopes, sinks, n_seq, t_new, past):
    buf_rows = buf.shape[2]
    smem = pl.BlockSpec(memory_space=pltpu.SMEM)
    return pl.pallas_call(
        functools.partial(_swa_sample_kernel, t_new, past),
        grid=(n_seq,),
        in_specs=[smem, smem,
                  pl.BlockSpec((t_new, Q_W), lambda b: (row0 // t_new + b, 0)),
                  pl.BlockSpec((t_new, KV_W), lambda b: (row0 // t_new + b, B_KV // KV_W)),
                  pl.BlockSpec((None, None, buf_rows, HEAD_DIM), lambda b: (lk, b, 0, 0))],
        out_specs=pl.BlockSpec((t_new, Q_W), lambda b: (b, 0)),
        out_shape=jax.ShapeDtypeStruct((n_seq * t_new, Q_W), F32),
        compiler_params=_params(("parallel",), 32),
        name="swa_sample",
    )(slopes, sinks, h, h, buf)


_STAIR = [PEER_TOPK // (k + 1) for k in range(PEER_TOPK)]
_STAIR_ROWS = -(-sum(_STAIR) // 8) * 8


def _top_values(work, out_ref):
    for k in range(PEER_TOPK):
        mx = jnp.max(work, axis=0, keepdims=True)
        out_ref[k:k + 1, :] = mx
        work = jnp.where(work == mx, -jnp.inf, work)


def _peer_route_kernel(x_ref, wqt_ref, sk_ref, xt_ref, b_ref, c_ref, ea_ref, b0_ref,
                       ta_ref, tb_ref, tc_ref, ts_ref, qt_ref):
    xt = x_ref[...].T.astype(BF16)
    xt_ref[...] = xt
    qt_ref[...] = _dot(wqt_ref[...], xt).astype(BF16)
    tc_ref[...] = jnp.full(tc_ref.shape, -jnp.inf, F32)
    n_col = x_ref.shape[0] // LANES

    for hd in range(PEER_HEADS):
        r0 = 2 * hd * PEER_HALF
        b_ref[hd] = _dot(sk_ref[2 * hd + 1], qt_ref[r0 + PEER_HALF:r0 + 2 * PEER_HALF])
        ea_ref[hd] = _dot(sk_ref[2 * hd], qt_ref[r0:r0 + PEER_HALF])

        def column(col, _, hd=hd):
            cs = pl.ds(pl.multiple_of(col * LANES, LANES), LANES)
            s_a = ea_ref[hd, :, cs]
            _top_values(s_a, ta_ref)
            _top_values(b_ref[hd, :, cs], tb_ref)
            top_b = tb_ref[...]
            off = 0
            for k, n_k in enumerate(_STAIR):
                tc_ref[off:off + n_k, :] = ta_ref[k:k + 1, :] + top_b[0:n_k]
                off += n_k
            _top_values(tc_ref[...], ts_ref)
            sc = ts_ref[...]
            thr = sc[PEER_TOPK - 1:PEER_TOPK]
            z = jnp.sum(jnp.exp(sc - sc[0:1]), axis=0, keepdims=True)
            cut_full = jnp.full(s_a.shape, jnp.inf, F32)
            for k, n_k in enumerate(_STAIR):
                a_k = ta_ref[k:k + 1, :]
                cut = jnp.min(jnp.where(a_k + top_b[0:n_k] >= thr, top_b[0:n_k], jnp.inf), axis=0, keepdims=True)
                cut_full = jnp.where(s_a == a_k, cut, cut_full)
            c_ref[hd, :, cs] = cut_full
            ea_ref[hd, :, cs] = jnp.exp(s_a - ta_ref[0:1, :]) / z
            b0_ref[hd:hd + 1, cs] = tb_ref[0:1, :]
            return 0

        lax.fori_loop(0, n_col, column, 0)


def _peer_route(x, wqt, sk, tb):
    n, d = x.shape
    big = jax.ShapeDtypeStruct((PEER_HEADS, PEER_KEYS, n), F32)
    big_spec = pl.BlockSpec((PEER_HEADS, PEER_KEYS, tb), lambda i: (0, 0, i))
    top = pltpu.VMEM((PEER_TOPK, LANES), F32)
    return pl.pallas_call(
        _peer_route_kernel,
        grid=(n // tb,),
        in_specs=[pl.BlockSpec((tb, d), lambda i: (i, 0)), pl.BlockSpec(wqt.shape, lambda i: (0, 0)),
                  pl.BlockSpec(sk.shape, lambda i: (0, 0, 0))],
        out_specs=[pl.BlockSpec((d, tb), lambda i: (0, i))] + [big_spec] * 3 + [
            pl.BlockSpec((PEER_HEADS, tb), lambda i: (0, i))],
        out_shape=[jax.ShapeDtypeStruct((d, n), BF16)] + [big] * 3 + [jax.ShapeDtypeStruct((PEER_HEADS, n), F32)],
        scratch_shapes=[top, top, pltpu.VMEM((_STAIR_ROWS, LANES), F32), top, pltpu.VMEM((wqt.shape[0], tb), BF16)],
        compiler_params=_params(("parallel",), 48),
        name="peer_route",
    )(x, wqt, sk)


def _peer_expert_kernel(alpha, ec, n_chunks, x_ref, xt_ref, b_ref, c_ref, ea_ref, b0_ref, u_ref, vt_ref, g_ref,
                        beta_ref, o_ref, acc_ref, eb_ref, act_a, act_b, hid_a, hid_b):
    j = pl.program_id(1)
    tb = x_ref.shape[0]
    per = ec // PEER_KEYS

    @pl.when(j == 0)
    def _():
        acc_ref[...] = jnp.zeros(acc_ref.shape, F32)
        for ref in (act_a, act_b, hid_a, hid_b):
            ref[...] = jnp.zeros(ref.shape, ref.dtype)
        for hd in range(PEER_HEADS):
            eb_ref[hd] = jnp.exp(b_ref[hd] - b0_ref[hd:hd + 1, :])

    def stages(act_w, act_r, hid_w, hid_r):
        k = j - 1
        valid = jnp.where(jnp.logical_and(k >= 0, k < n_chunks), 1.0, 0.0)
        i0 = jnp.clip(k, 0, n_chunks - 1) * per

        def act_piece(h):
            cols = slice(h * MXU_N, (h + 1) * MXU_N)
            act_w[:, cols] = _dot(u_ref[...], xt_ref[:, cols])

        def acc_piece(r):
            rows = slice(r * ACC_ROWS, (r + 1) * ACC_ROWS)
            acc_ref[rows, :] += _dot(vt_ref[rows, :], hid_r[...])

        c_rows = [[c_ref[hd, pl.ds(i0 + ii, 1), :] for hd in range(PEER_HEADS)] for ii in range(per)]
        ea_rows = [[ea_ref[hd, pl.ds(i0 + ii, 1), :] * valid for hd in range(PEER_HEADS)] for ii in range(per)]

        def gate_tile(col, half):
            cs = slice(col * LANES, (col + 1) * LANES)
            ks = slice(half * GATE_ROWS, (half + 1) * GATE_ROWS)
            w = [None] * per
            for hd in range(PEER_HEADS):
                b = b_ref[hd, ks, cs]
                eb = eb_ref[hd, ks, cs]
                for ii in range(per):
                    t = jnp.where(b >= c_rows[ii][hd][:, cs], eb, 0.0) * ea_rows[ii][hd][:, cs]
                    w[ii] = t if w[ii] is None else w[ii] + t
            for ii in range(per):
                rows = slice(ii * PEER_KEYS + half * GATE_ROWS, ii * PEER_KEYS + (half + 1) * GATE_ROWS)
                act = act_r[rows, cs]
                hid_w[rows, cs] = (0.5 * act * (1.0 + lax.erf(act * SQRT_HALF)) * w[ii]).astype(BF16)

        mxu = [functools.partial(act_piece, h) for h in range(tb // MXU_N)]
        mxu += [functools.partial(acc_piece, r) for r in range(acc_ref.shape[0] // ACC_ROWS)]
        tiles = [(col, half) for col in range(tb // LANES) for half in range(PEER_KEYS // GATE_ROWS)]
        for t, (col, half) in enumerate(tiles):
            for piece in mxu[t * len(mxu) // len(tiles):(t + 1) * len(mxu) // len(tiles)]:
                piece()
            gate_tile(col, half)

    @pl.when(j % 2 == 0)
    def _():
        stages(act_a, act_b, hid_b, hid_a)

    @pl.when(j % 2 == 1)
    def _():
        stages(act_b, act_a, hid_a, hid_b)

    @pl.when(j == pl.num_programs(1) - 1)
    def _():
        o_ref[...] = _layer_norm(alpha * x_ref[...] + acc_ref[...].T, g_ref[...], beta_ref[...])


def _peer_experts(x, route, u, vt, layer, g, beta, alpha, tb, ec):
    n, d = x.shape
    ne = u.shape[1]
    xt, b, c, ea, b0 = route
    n_chunks = ne // ec
    big_spec = pl.BlockSpec((PEER_HEADS, PEER_KEYS, tb), lambda i, j: (0, 0, i))
    act = pltpu.VMEM((ec, tb), F32)
    hid = pltpu.VMEM((ec, tb), BF16)
    return pl.pallas_call(
        functools.partial(_peer_expert_kernel, alpha, ec, n_chunks),
        grid=(n // tb, n_chunks + 2),
        in_specs=[pl.BlockSpec((tb, d), lambda i, j: (i, 0)), pl.BlockSpec((d, tb), lambda i, j: (0, i))]
        + [big_spec] * 3 + [
            pl.BlockSpec((PEER_HEADS, tb), lambda i, j: (0, i)),
            pl.BlockSpec((None, ec, d), lambda i, j: (layer, jnp.minimum(j, n_chunks - 1), 0)),
            pl.BlockSpec((None, d, ec), lambda i, j: (layer, 0, jnp.clip(j - 2, 0, n_chunks - 1))),
            pl.BlockSpec((1, d), lambda i, j: (0, 0)), pl.BlockSpec((1, d), lambda i, j: (0, 0))],
        out_specs=pl.BlockSpec((tb, d), lambda i, j: (i, 0)),
        out_shape=jax.ShapeDtypeStruct((n, d), F32),
        scratch_shapes=[pltpu.VMEM((d, tb), F32), pltpu.VMEM((PEER_HEADS, PEER_KEYS, tb), F32), act, act, hid, hid],
        compiler_params=_params(("parallel", "arbitrary"), 56),
        name="peer_experts",
    )(x, xt, b, c, ea, b0, u, vt, g, beta)


def _overlap_table(n_cmp_pad, n_blk, n_lanes):
    c0 = jnp.arange(n_cmp_pad)[:, None] * CMP_STRIDE
    s0 = jnp.arange(n_lanes)[None, :] * SEL_BLK
    ov = (c0 <= s0 + SEL_BLK - 1) & (c0 + CMP_LEN - 1 >= s0) & (jnp.arange(n_lanes)[None, :] < n_blk)
    return ov.astype(BF16)


def _expand_table(n_lanes, n_pos):
    return (jnp.arange(n_lanes)[:, None] == jnp.arange(n_pos)[None, :] // SEL_BLK).astype(BF16)


def _alibi_slopes():
    i = jnp.arange(1, N_HEADS + 1, dtype=F32)
    return jnp.exp2(-8.0 * i / N_HEADS)


def _round_up(x, m):
    return -(-x // m) * m


def kernel(x_prompt, x_sample, mem_prompt, cache_a_cmp_kv, cache_a_sel_kv, cache_a_win_kv, cache_b_kv, cache_mem_kv, page_table, w_in_a, b_gate_a, w_cmp_a, pe_cmp_a, w_in_b, sinks_b, w_mem_kv, w_out, b_out, ln_g, ln_b, peer_wq, peer_subkeys, peer_u, peer_v):
    batch, seq, d = x_prompt.shape
    n_seq, t_new, _ = x_sample.shape
    depth = w_out.shape[0]
    past = page_table.shape[1] * PAGE_SIZE
    n_prompt = batch * seq
    n_tok = n_prompt + n_seq * t_new
    alpha = (2.0 * depth) ** 0.25
    assert past % SEL_BLK == 0 and t_new <= SEL_BLK and n_tok % TB_PEER == 0 and n_tok % TM_PROJ == 0
    tq = min(TQ_ATTN, seq)
    slopes = _alibi_slopes()

    x = jnp.concatenate([x_prompt.reshape(n_prompt, d), x_sample.reshape(n_seq * t_new, d)], axis=0)
    kv_shape = lambda n, t, g: (n, t, 2, g, HEAD_DIM)
    a_cmp_p, a_cmp_s, a_sel_p, a_sel_s, a_win_p, a_win_s, b_kv_p, b_kv_s, mem_p = ([] for _ in range(9))

    n_pool = cache_a_cmp_kv.shape[1]
    cmp_pages = cache_a_cmp_kv.reshape(-1, n_pool, PAGE_SIZE // CMP_STRIDE, CMP_STRIDE * KV_ROWS, HEAD_DIM)
    sel_pages = cache_a_sel_kv.reshape(-1, n_pool, PAGE_SIZE * KV_ROWS, HEAD_DIM)
    win_rows = cache_a_win_kv.reshape(cache_a_win_kv.shape[0], n_seq, -1, HEAD_DIM)
    buf_rows = cache_b_kv.reshape(cache_b_kv.shape[0], n_seq, -1, HEAD_DIM)
    mem_rows = cache_mem_kv.reshape(depth, n_seq, -1, HEAD_DIM)
    peer_ub = peer_u.astype(BF16)
    peer_vt = jnp.swapaxes(peer_v, 1, 2).astype(BF16)

    def shifted_rows(rows, new):
        new = new.reshape(n_seq, t_new * KV_ROWS, HEAD_DIM)
        out = jnp.concatenate([rows[:, t_new * KV_ROWS:], new], axis=1)
        return out.reshape(kv_shape(n_seq, rows.shape[1] // KV_ROWS, N_KV))

    for li in range(depth):
        lk = li // 2
        n_mem = mem_prompt.shape[1]
        mkv_p = _matmul(mem_prompt.reshape(-1, d), w_mem_kv[li].astype(BF16), batch * n_mem, 2 * QM_W)
        mem_p.append(mkv_p.reshape(batch, n_mem, 2, N_MEM_HEADS, HEAD_DIM))
        mkv_p = mkv_p.reshape(1, batch, n_mem, 2 * QM_W)

        if li % 2 == 0:
            w_in = w_in_a[lk]
            g0 = Q_W + 3 * KV_W
            gate_w = w_in[:, g0:g0 + 3 * N_HEADS].reshape(d, N_KV, 3 * HPG)
            gate_w = jnp.pad(gate_w, ((0, 0), (0, 0), (0, LANES - 3 * HPG))).reshape(d, N_KV * LANES)
            w_r = jnp.concatenate([w_in[:, :g0], w_in[:, g0 + 3 * N_HEADS:], gate_w], axis=1).astype(BF16)
            bg = jnp.pad(b_gate_a[lk].reshape(N_KV, 3 * HPG), ((0, 0), (0, LANES - 3 * HPG))).reshape(1, N_KV * LANES)
            h = _matmul(x, w_r, TM_PROJ, A_WIDTH // 6)

            w_cmp = w_cmp_a[lk].astype(BF16)
            wab = jnp.concatenate([w_cmp[:, :CMP_STRIDE], w_cmp[:, CMP_STRIDE:]], axis=-1)
            wab = wab.reshape(2, CMP_STRIDE // 2, 2 * HEAD_DIM, 2 * HEAD_DIM)
            cmp_bias = _cmp_bias(pe_cmp_a[lk], w_cmp)
            hp = h[:n_prompt].reshape(batch, seq, -1)
            hs = h[n_prompt:].reshape(n_seq, t_new, -1)
            a_cmp_p.append(hp[..., A_KC:A_KC + KV_W].reshape(kv_shape(batch, seq, N_KV)))
            kvc_p = _compress_prompt(a_cmp_p[-1].reshape(-1, HEAD_DIM), wab, cmp_bias, batch, seq)
            n_blk_p = seq // SEL_BLK
            ov_p = _overlap_table(seq // CMP_STRIDE, n_blk_p, _round_up(n_blk_p, LANES))
            ex_p = _expand_table(_round_up(n_blk_p, LANES), seq)
            mix_p = _nsa_prompt(h, kvc_p, slopes, bg, ov_p, ex_p, batch, seq, tq)

            n_blk_s = -(-(past + t_new) // SEL_BLK)
            nbl = _round_up(n_blk_s, LANES)
            ov_s = _overlap_table(past // CMP_STRIDE, n_blk_s, nbl)
            ex_s = _expand_table(nbl, past)
            oc_s, sel_s = _nsa_sample_cmp(h, n_prompt, cmp_pages, lk, page_table, slopes, wab, cmp_bias, ov_s,
                                          n_seq, t_new, past)
            mix_s = _nsa_sample_sel(h, n_prompt, sel_pages, lk, page_table, slopes, sel_s, ex_s, oc_s, bg, win_rows,
                                    n_seq, t_new, past)

            a_cmp_s.append(hs[..., A_KC:A_KC + KV_W].reshape(kv_shape(n_seq, t_new, N_KV)))
            a_sel_p.append(hp[..., A_KS:A_KS + KV_W].reshape(kv_shape(batch, seq, N_KV)))
            a_sel_s.append(hs[..., A_KS:A_KS + KV_W].reshape(kv_shape(n_seq, t_new, N_KV)))
            wn = min(WIN_A, seq)
            a_win_p.append(hp[:, seq - wn:, A_KW:A_KW + KV_W].reshape(kv_shape(batch, wn, N_KV)))
            a_win_s.append(shifted_rows(win_rows[lk], hs[..., A_KW:A_KW + KV_W]))
            qm_col = A_QM
        else:
            h = _matmul(x, w_in_b[lk].astype(BF16), TM_PROJ, B_WIDTH // 4)
            mix_p = _swa_prompt(h, slopes, sinks_b[lk], batch, seq, tq)
            mix_s = _swa_sample(h, n_prompt, buf_rows, lk, slopes, sinks_b[lk], n_seq, t_new, past)
            hp = h[:n_prompt].reshape(batch, seq, -1)
            hs = h[n_prompt:].reshape(n_seq, t_new, -1)
            wn = min(WIN_B, seq)
            b_kv_p.append(hp[:, seq - wn:, B_KV:B_KV + KV_W].reshape(kv_shape(batch, wn, N_KV)))
            b_kv_s.append(shifted_rows(buf_rows[lk], hs[..., B_KV:B_KV + KV_W]))
            qm_col = B_QM

        mem_o_p = _mem_attend(h, qm_col, 0, mkv_p, 0, tq, seq // tq)
        mem_o_s = _mem_attend(h, qm_col, n_prompt, mem_rows, li, t_new, 1, math.gcd(n_seq, MEM_SEQS_PER_STEP))
        mix = jnp.concatenate([mix_p, mix_s], axis=0)
        mem_o = jnp.concatenate([mem_o_p, mem_o_s], axis=0)
        w_o = w_out[li].astype(BF16)
        x = _outproj_ln(mix, mem_o, w_o[:Q_W], w_o[Q_W:], b_out[li].reshape(1, d), x,
                        ln_g[li, 0].reshape(1, d), ln_b[li, 0].reshape(1, d), alpha, TM_OUT)

        sk = peer_subkeys[li].reshape(2 * PEER_HEADS, PEER_KEYS, PEER_HALF).astype(BF16)
        route = _peer_route(x, peer_wq[li].T.astype(BF16), sk, TB_PEER)
        x = _peer_experts(x, route, peer_ub, peer_vt, li, ln_g[li, 1].reshape(1, d), ln_b[li, 1].reshape(1, d),
                          alpha, TB_PEER, EC_PEER)

    return (x[:n_prompt].reshape(batch, seq, d), x[n_prompt:].reshape(n_seq, t_new, d),
            jnp.stack(a_cmp_p), jnp.stack(a_cmp_s), jnp.stack(a_sel_p), jnp.stack(a_sel_s),
            jnp.stack(a_win_p), jnp.stack(a_win_s), jnp.stack(b_kv_p), jnp.stack(b_kv_s), jnp.stack(mem_p))
```

```python
import functools
import math

import jax
import jax.numpy as jnp
from jax import lax
from jax.experimental import pallas as pl
from jax.experimental.pallas import tpu as pltpu

F32 = jnp.float32
BF16 = jnp.bfloat16

HEAD_DIM = 128
N_MEM_HEADS = 4
N_KV = 2
HPG = 6
N_HEADS = N_KV * HPG
Q_W = N_HEADS * HEAD_DIM
KV_W = 2 * N_KV * HEAD_DIM
KV_ROWS = 2 * N_KV
QM_W = N_MEM_HEADS * HEAD_DIM
CMP_STRIDE = 16
CMP_LEN = 32
SEL_BLK = 64
N_SEL = 16
WIN_A = 512
WIN_B = 128
PAGE_SIZE = 128
PEER_HEADS = 8
PEER_KEYS = 128
PEER_TOPK = 16
PEER_HALF = 128
LN_EPS = 1e-5
NEG_INF = -1e30
SEL_FORCE = 1e4
SCALE = HEAD_DIM ** -0.5
SQRT_HALF = math.sqrt(0.5)
LANES = 128
MXU_N = 256
ACC_ROWS = 256
GATE_ROWS = 16

TM_PROJ = 1024
TM_OUT = 512
TQ_ATTN = 256
TB_PEER = 512
EC_PEER = 512
PAGES_PER_STEP = 32
MEM_SEQS_PER_STEP = 8

A_KC, A_KS, A_KW, A_QM, A_GATE = Q_W, Q_W + KV_W, Q_W + 2 * KV_W, Q_W + 3 * KV_W, Q_W + 3 * KV_W + QM_W
A_WIDTH = A_GATE + N_KV * LANES
B_KV, B_QM = Q_W, Q_W + KV_W
B_WIDTH = B_QM + QM_W


def _params(sem, vmem_mb, flags=None):
    return pltpu.CompilerParams(dimension_semantics=sem, vmem_limit_bytes=vmem_mb << 20, flags=flags)


def _dot(a, b):
    return jnp.dot(a, b, preferred_element_type=F32)


def _dot_nt(a, b):
    return lax.dot_general(a, b, (((1,), (1,)), ((), ())), preferred_element_type=F32)


def _mm_kernel(x_ref, w_ref, o_ref):
    o_ref[...] = _dot(x_ref[...].astype(BF16), w_ref[...])


def _matmul(x, w, tm, tn):
    m, k = x.shape
    n = w.shape[1]
    return pl.pallas_call(
        _mm_kernel,
        grid=(m // tm, n // tn),
        in_specs=[pl.BlockSpec((tm, k), lambda i, j: (i, 0)), pl.BlockSpec((k, tn), lambda i, j: (0, j))],
        out_specs=pl.BlockSpec((tm, tn), lambda i, j: (i, j)),
        out_shape=jax.ShapeDtypeStruct((m, n), F32),
        compiler_params=_params(("parallel", "parallel"), 40),
        name="proj",
    )(x, w)


def _layer_norm(z, g, b):
    mu = jnp.mean(z, axis=-1, keepdims=True)
    zc = z - mu
    var = jnp.mean(zc * zc, axis=-1, keepdims=True)
    return zc * lax.rsqrt(var + LN_EPS) * g + b


def _outproj_kernel(alpha, mix_ref, mem_ref, w1_ref, w2_ref, bias_ref, x_ref, g_ref, b_ref, o_ref):
    y = _dot(mix_ref[...].astype(BF16), w1_ref[...]) + _dot(mem_ref[...].astype(BF16), w2_ref[...]) + bias_ref[...]
    o_ref[...] = _layer_norm(alpha * x_ref[...] + y, g_ref[...], b_ref[...])


def _outproj_ln(mix, mem, w1, w2, bias, x, g, b, alpha, tm):
    m, d = x.shape
    row = lambda i: (i, 0)
    fixed = lambda i: (0, 0)
    return pl.pallas_call(
        functools.partial(_outproj_kernel, alpha),
        grid=(m // tm,),
        in_specs=[pl.BlockSpec((tm, Q_W), row), pl.BlockSpec((tm, QM_W), row),
                  pl.BlockSpec((Q_W, d), fixed), pl.BlockSpec((QM_W, d), fixed), pl.BlockSpec((1, d), fixed),
                  pl.BlockSpec((tm, d), row), pl.BlockSpec((1, d), fixed), pl.BlockSpec((1, d), fixed)],
        out_specs=pl.BlockSpec((tm, d), row),
        out_shape=jax.ShapeDtypeStruct((m, d), F32),
        compiler_params=_params(("parallel",), 48),
        name="outproj_ln",
    )(mix, mem, w1, w2, bias, x, g, b)


def _flash_init(rows):
    return (jnp.full((rows, 1), NEG_INF, F32), jnp.zeros((rows, 1), F32), jnp.zeros((rows, HEAD_DIM), F32))


def _flash_tile(carry, q, k, v, slope_col, qpos_col, kpos_row, window=None, extra=None, causal=True, guard=True):
    m, l, acc = carry
    dist = qpos_col - kpos_row
    s = _dot_nt(q, k) * SCALE - slope_col * dist.astype(F32)
    mask = None
    for cond in ((dist >= 0) if causal else None, (dist < window) if window is not None else None, extra):
        if cond is not None:
            mask = cond if mask is None else jnp.logical_and(mask, cond)
    if mask is not None:
        s = jnp.where(mask, s, NEG_INF)
    m_new = jnp.maximum(m, jnp.max(s, axis=1, keepdims=True))
    e = jnp.exp(s - m_new)
    if mask is not None and guard:
        e = jnp.where(mask, e, 0.0)
    corr = jnp.exp(m - m_new)
    l = l * corr + jnp.sum(e, axis=1, keepdims=True)
    acc = acc * corr + _dot(e.astype(BF16), v)
    return m_new, l, acc


def _flash_out(carry):
    _, l, acc = carry
    return acc / jnp.maximum(l, 1e-30)


def _flash_out_sink(carry, sink_col):
    m, l, acc = carry
    m_f = jnp.maximum(m, sink_col)
    corr = jnp.exp(m - m_f)
    return acc * corr / (l * corr + jnp.exp(sink_col - m_f))


def _stack_heads(q_ref, nq):
    return jnp.concatenate([q_ref[:, hh * HEAD_DIM:(hh + 1) * HEAD_DIM] for hh in range(HPG)], axis=0).astype(BF16)


def _head_cols(ref, base, nq):
    return jnp.concatenate([jnp.full((nq, 1), ref[base + hh], F32) for hh in range(HPG)], axis=0)


def _stacked_pos(q0, nq):
    pos = q0 + lax.broadcasted_iota(jnp.int32, (nq, 1), 0)
    return jnp.concatenate([pos] * HPG, axis=0)


def _tile_rows(x):
    return jnp.concatenate([x] * HPG, axis=0)


def _top_blocks(score, lane):
    sel = jnp.zeros_like(score)
    work = score
    for _ in range(N_SEL):
        hit = lane == jnp.argmax(work, axis=1, keepdims=True).astype(jnp.int32)
        sel = jnp.where(hit, 1.0, sel)
        work = jnp.where(hit, -jnp.inf, work)
    return sel


def _cmp_branch(q_all, kc, vc, slope_col, qpos_col, ov_ref, nq, n_blk):
    ncp = kc.shape[0]
    end_row = lax.broadcasted_iota(jnp.int32, (1, ncp), 1) * CMP_STRIDE + (CMP_LEN - 1)
    dist = qpos_col - end_row
    mask = dist >= 0
    s = jnp.where(mask, _dot_nt(q_all, kc) * SCALE - slope_col * dist.astype(F32), NEG_INF)
    e = jnp.where(mask, jnp.exp(s - jnp.max(s, axis=1, keepdims=True)), 0.0)
    p = e / jnp.maximum(jnp.sum(e, axis=1, keepdims=True), 1e-30)
    o_c = _dot(p.astype(BF16), vc)
    imp_c = p[0:nq]
    for hh in range(1, HPG):
        imp_c = imp_c + p[hh * nq:(hh + 1) * nq]
    hi = imp_c.astype(BF16)
    lo = (imp_c - hi.astype(F32)).astype(BF16)
    imp = _dot(hi, ov_ref[...]) + _dot(lo, ov_ref[...])
    nl = imp.shape[1]
    lane_i = lax.broadcasted_iota(jnp.int32, (nq, nl), 1)
    cur = qpos_col[0:nq] // SEL_BLK
    forced = jnp.logical_or(lane_i == 0, jnp.logical_or(lane_i == cur, lane_i == cur - 1))
    score = jnp.where(forced, SEL_FORCE, jnp.where(lane_i <= cur, imp, -SEL_FORCE))
    score = jnp.where(lane_i < n_blk, score, -jnp.inf)
    return o_c, _top_blocks(score, lane_i)


def _gated_sum(gate_ref, bg_ref, branches, o_ref, nq):
    gates = jax.nn.sigmoid(gate_ref[...] + bg_ref[...])
    for hh in range(HPG):
        rows = slice(hh * nq, (hh + 1) * nq)
        o = gates[:, 3 * hh:3 * hh + 1] * branches[0][rows]
        for j in (1, 2):
            o = o + gates[:, 3 * hh + j:3 * hh + j + 1] * branches[j][rows]
        o_ref[:, hh * HEAD_DIM:(hh + 1) * HEAD_DIM] = o


def _chunk_ab(load_rows, wab_ref, c):
    acc = None
    for p in range(CMP_STRIDE // 2):
        x = jnp.concatenate([load_rows(2 * p), load_rows(2 * p + 1)], axis=1).astype(BF16)
        t = _dot(x, wab_ref[c, p])
        acc = t if acc is None else acc + t
    return acc


def _cmp_bias_kernel(pe_ref, w_ref, o_ref):
    for c in range(2):
        acc = jnp.zeros((8, HEAD_DIM), F32)
        for r in range(CMP_LEN):
            acc = acc + _dot(jnp.broadcast_to(pe_ref[c, r:r + 1, :], (8, HEAD_DIM)).astype(BF16), w_ref[c, r])
        o_ref[c] = acc


def _cmp_bias(pe, w):
    return pl.pallas_call(
        _cmp_bias_kernel,
        out_shape=jax.ShapeDtypeStruct((2, 8, HEAD_DIM), F32),
        name="cmp_bias",
    )(pe, w)


def _blocks_from_ab(ab, bias):
    n = ab.shape[0]
    return ab[:, :HEAD_DIM] + pltpu.roll(ab[:, HEAD_DIM:], n - 1, 0) + bias


def _compress_kernel(nch, kv_ref, wab_ref, bias_ref, o_ref):
    for c in range(2):
        for g in range(N_KV):
            cg = c * N_KV + g
            ab = _chunk_ab(lambda r: kv_ref[pl.ds(r * KV_ROWS + cg, nch, stride=CMP_STRIDE * KV_ROWS), :], wab_ref, c)
            o_ref[c, g] = _blocks_from_ab(ab, bias_ref[c, 0:1, :])


def _compress_prompt(kv, wab, bias, batch, seq):
    nch = seq // CMP_STRIDE
    return pl.pallas_call(
        functools.partial(_compress_kernel, nch),
        grid=(batch,),
        in_specs=[pl.BlockSpec((seq * KV_ROWS, HEAD_DIM), lambda b: (b, 0)),
                  pl.BlockSpec(wab.shape, lambda b: (0, 0, 0, 0)), pl.BlockSpec(bias.shape, lambda b: (0, 0, 0))],
        out_specs=pl.BlockSpec((None, 2, N_KV, nch, HEAD_DIM), lambda b: (b, 0, 0, 0, 0)),
        out_shape=jax.ShapeDtypeStruct((batch, 2, N_KV, nch, HEAD_DIM), F32),
        compiler_params=_params(("parallel",), 40),
        name="compress_prompt",
    )(kv, wab, bias)


def _nsa_prompt_kernel(tq, seq, slopes_ref, q_ref, gate_ref, bg_ref, kvc_ref, ks_ref, vs_ref, kw_ref, vw_ref,
                       ov_ref, e_ref, o_ref, selx_ref):
    g = pl.program_id(1)
    qi = pl.program_id(2)
    q0 = qi * tq
    q_all = _stack_heads(q_ref, tq)
    slope_col = _head_cols(slopes_ref, g * HPG, tq)
    qpos_col = _stacked_pos(q0, tq)

    o_c, sel = _cmp_branch(q_all, kvc_ref[0].astype(BF16), kvc_ref[1].astype(BF16), slope_col, qpos_col, ov_ref,
                           tq, seq // SEL_BLK)
    selx_ref[...] = _dot(sel.astype(BF16), e_ref[...])

    def key_tile(k_ref, v_ref, kj):
        k0 = pl.multiple_of(kj * tq, tq)
        kpos_row = k0 + lax.broadcasted_iota(jnp.int32, (1, tq), 1)
        return k0, k_ref[pl.ds(k0, tq), :].astype(BF16), v_ref[pl.ds(k0, tq), :].astype(BF16), kpos_row

    def sel_tile(kj, carry, diagonal):
        k0, k, v, kpos_row = key_tile(ks_ref, vs_ref, kj)
        extra = _tile_rows(selx_ref[:, pl.ds(k0, tq)]) > 0.5
        return _flash_tile(carry, q_all, k, v, slope_col, qpos_col, kpos_row, extra=extra, causal=diagonal,
                           guard=False)

    def win_tile(kj, carry):
        _, k, v, kpos_row = key_tile(kw_ref, vw_ref, kj)
        return _flash_tile(carry, q_all, k, v, slope_col, qpos_col, kpos_row, window=WIN_A)

    rows = HPG * tq
    carry = lax.fori_loop(0, qi, functools.partial(sel_tile, diagonal=False), _flash_init(rows))
    o_s = _flash_out(sel_tile(qi, carry, True))
    win_lo = jnp.maximum(qi - (WIN_A + tq - 2) // tq, 0)
    o_w = _flash_out(lax.fori_loop(win_lo, qi + 1, win_tile, _flash_init(rows)))
    _gated_sum(gate_ref, bg_ref, (o_c, o_s, o_w), o_ref, tq)


def _nsa_prompt(h, kvc, slopes, bg, ov, expand, batch, seq, tq):
    nq = seq // tq
    nch = kvc.shape[3]
    kv_spec = lambda col: pl.BlockSpec((seq, HEAD_DIM), lambda b, g, i, col=col: (b, col // HEAD_DIM + g))
    return pl.pallas_call(
        functools.partial(_nsa_prompt_kernel, tq, seq),
        grid=(batch, N_KV, nq),
        in_specs=[pl.BlockSpec(memory_space=pltpu.SMEM),
                  pl.BlockSpec((tq, HPG * HEAD_DIM), lambda b, g, i: (b * nq + i, g)),
                  pl.BlockSpec((tq, LANES), lambda b, g, i: (b * nq + i, A_GATE // LANES + g)),
                  pl.BlockSpec((1, LANES), lambda b, g, i: (0, g)),
                  pl.BlockSpec((None, 2, None, nch, HEAD_DIM), lambda b, g, i: (b, 0, g, 0, 0)),
                  kv_spec(A_KS), kv_spec(A_KS + N_KV * HEAD_DIM), kv_spec(A_KW), kv_spec(A_KW + N_KV * HEAD_DIM),
                  pl.BlockSpec(ov.shape, lambda b, g, i: (0, 0)),
                  pl.BlockSpec(expand.shape, lambda b, g, i: (0, 0))],
        out_specs=pl.BlockSpec((tq, HPG * HEAD_DIM), lambda b, g, i: (b * nq + i, g)),
        out_shape=jax.ShapeDtypeStruct((batch * seq, Q_W), F32),
        scratch_shapes=[pltpu.VMEM((tq, seq), F32)],
        compiler_params=_params(("parallel", "parallel", "arbitrary"), 48),
        name="nsa_prompt",
    )(slopes, h, h, bg, kvc, h, h, h, h, ov, expand)


def _swa_prompt_kernel(tq, slopes_ref, sinks_ref, q_ref, k_ref, v_ref, o_ref):
    g = pl.program_id(1)
    qi = pl.program_id(2)
    q_all = _stack_heads(q_ref, tq)
    slope_col = _head_cols(slopes_ref, g * HPG, tq)
    sink_col = _head_cols(sinks_ref, g * HPG, tq)
    qpos_col = _stacked_pos(qi * tq, tq)

    def tile(kj, carry):
        k0 = pl.multiple_of(kj * tq, tq)
        kpos_row = k0 + lax.broadcasted_iota(jnp.int32, (1, tq), 1)
        k = k_ref[pl.ds(k0, tq), :].astype(BF16)
        v = v_ref[pl.ds(k0, tq), :].astype(BF16)
        return _flash_tile(carry, q_all, k, v, slope_col, qpos_col, kpos_row, window=WIN_B)

    lo = jnp.maximum(qi - (WIN_B + tq - 2) // tq, 0)
    o = _flash_out_sink(lax.fori_loop(lo, qi + 1, tile, _flash_init(HPG * tq)), sink_col)
    for hh in range(HPG):
        o_ref[:, hh * HEAD_DIM:(hh + 1) * HEAD_DIM] = o[hh * tq:(hh + 1) * tq]


def _swa_prompt(h, slopes, sinks, batch, seq, tq):
    nq = seq // tq
    kv_spec = lambda col: pl.BlockSpec((seq, HEAD_DIM), lambda b, g, i, col=col: (b, col // HEAD_DIM + g))
    smem = pl.BlockSpec(memory_space=pltpu.SMEM)
    return pl.pallas_call(
        functools.partial(_swa_prompt_kernel, tq),
        grid=(batch, N_KV, nq),
        in_specs=[smem, smem,
                  pl.BlockSpec((tq, HPG * HEAD_DIM), lambda b, g, i: (b * nq + i, g)),
                  kv_spec(B_KV), kv_spec(B_KV + N_KV * HEAD_DIM)],
        out_specs=pl.BlockSpec((tq, HPG * HEAD_DIM), lambda b, g, i: (b * nq + i, g)),
        out_shape=jax.ShapeDtypeStruct((batch * seq, Q_W), F32),
        compiler_params=_params(("parallel", "parallel", "arbitrary"), 40),
        name="swa_prompt",
    )(slopes, sinks, h, h, h)


def _mem_kernel(row_layout, q_ref, kv_ref, o_ref):
    n_sub = kv_ref.shape[0]
    tq = q_ref.shape[0] // n_sub
    for sub in range(n_sub):
        rows = slice(sub * tq, (sub + 1) * tq)
        for hd in range(N_MEM_HEADS):
            cols = slice(hd * HEAD_DIM, (hd + 1) * HEAD_DIM)
            if row_layout:
                n_mem = kv_ref.shape[1] // (2 * N_MEM_HEADS)
                k = kv_ref[sub, pl.ds(hd, n_mem, stride=2 * N_MEM_HEADS), :].astype(BF16)
                v = kv_ref[sub, pl.ds(N_MEM_HEADS + hd, n_mem, stride=2 * N_MEM_HEADS), :].astype(BF16)
            else:
                k = kv_ref[sub, :, hd * HEAD_DIM:(hd + 1) * HEAD_DIM].astype(BF16)
                v = kv_ref[sub, :, QM_W + hd * HEAD_DIM:QM_W + (hd + 1) * HEAD_DIM].astype(BF16)
            s = _dot_nt(q_ref[rows, cols].astype(BF16), k) * SCALE
            e = jnp.exp(s - jnp.max(s, axis=1, keepdims=True))
            p = e / jnp.sum(e, axis=1, keepdims=True)
            o_ref[rows, cols] = _dot(p.astype(BF16), v)


def _mem_attend(h, qm_col, row0, mkv, layer, tq, blocks_per_batch, n_sub=1):
    _, n_batch, kv_rows, kv_cols = mkv.shape
    assert n_sub == 1 or blocks_per_batch == 1
    nblk = n_batch * blocks_per_batch // n_sub
    rows = tq * n_sub
    return pl.pallas_call(
        functools.partial(_mem_kernel, kv_cols == HEAD_DIM),
        grid=(nblk,),
        in_specs=[pl.BlockSpec((rows, QM_W), lambda i: (row0 // rows + i, qm_col // QM_W)),
                  pl.BlockSpec((None, n_sub, kv_rows, kv_cols), lambda i: (layer, i // blocks_per_batch, 0, 0))],
        out_specs=pl.BlockSpec((rows, QM_W), lambda i: (i, 0)),
        out_shape=jax.ShapeDtypeStruct((nblk * rows, QM_W), F32),
        compiler_params=_params(("parallel",), 32),
        name="mem_attend",
    )(h, mkv)


def _nsa_sample_cmp_kernel(t_new, past, *refs):
    npg = PAGES_PER_STEP
    _, slopes_ref = refs[0], refs[1]
    page_refs = refs[2:2 + npg]
    q_ref, wab_ref, bias_ref, ov_ref, oc_ref, sel_ref, ab_ref = refs[2 + npg:]
    j = pl.program_id(1)
    rows_per_step = npg * PAGE_SIZE // CMP_STRIDE
    pages = [jnp.swapaxes(p[...], 0, 1) for p in page_refs]
    for c in range(2):
        for g in range(N_KV):
            cg = c * N_KV + g
            ab = _chunk_ab(lambda r: jnp.concatenate([x[r * KV_ROWS + cg] for x in pages], axis=0), wab_ref, c)
            ab_ref[cg, pl.ds(pl.multiple_of(j * rows_per_step, rows_per_step), rows_per_step), :] = ab

    @pl.when(j == pl.num_programs(1) - 1)
    def _():
        bias = [bias_ref[c, 0:1, :] for c in range(2)]
        n_blk = -(-(past + t_new) // SEL_BLK)
        for g in range(N_KV):
            kc = _blocks_from_ab(ab_ref[g], bias[0]).astype(BF16)
            vc = _blocks_from_ab(ab_ref[N_KV + g], bias[1]).astype(BF16)
            q_all = _stack_heads(q_ref.at[:, g * HPG * HEAD_DIM:(g + 1) * HPG * HEAD_DIM], t_new)
            slope_col = _head_cols(slopes_ref, g * HPG, t_new)
            qpos_col = _stacked_pos(past, t_new)
            o_c, sel = _cmp_branch(q_all, kc, vc, slope_col, qpos_col, ov_ref, t_new, n_blk)
            for hh in range(HPG):
                oc_ref[:, (g * HPG + hh) * HEAD_DIM:(g * HPG + hh + 1) * HEAD_DIM] = o_c[hh * t_new:(hh + 1) * t_new]
            sel_ref[g] = sel


def _nsa_sample_cmp(h, row0, cache, lk, page_table, slopes, wab, bias, ov, n_seq, t_new, past):
    n_pages = past // PAGE_SIZE
    npg = PAGES_PER_STEP
    nch = past // CMP_STRIDE
    page_spec = lambda k: pl.BlockSpec((None, None, PAGE_SIZE // CMP_STRIDE, CMP_STRIDE * KV_ROWS, HEAD_DIM),
                                       lambda b, j, pt, k=k: (lk, pt[b, j * npg + k], 0, 0, 0))
    fixed = lambda nd: (lambda b, j, pt: (0,) * nd)
    grid_spec = pltpu.PrefetchScalarGridSpec(
        num_scalar_prefetch=1,
        grid=(n_seq, n_pages // npg),
        in_specs=[pl.BlockSpec(memory_space=pltpu.SMEM)] + [page_spec(k) for k in range(npg)] + [
            pl.BlockSpec((t_new, Q_W), lambda b, j, pt: (row0 // t_new + b, 0)),
            pl.BlockSpec(wab.shape, fixed(4)), pl.BlockSpec(bias.shape, fixed(3)), pl.BlockSpec(ov.shape, fixed(2))],
        out_specs=[pl.BlockSpec((t_new, Q_W), lambda b, j, pt: (b, 0)),
                   pl.BlockSpec((None, N_KV, t_new, ov.shape[1]), lambda b, j, pt: (b, 0, 0, 0))],
        scratch_shapes=[pltpu.VMEM((2 * N_KV, nch, 2 * HEAD_DIM), F32)],
    )
    return pl.pallas_call(
        functools.partial(_nsa_sample_cmp_kernel, t_new, past),
        grid_spec=grid_spec,
        out_shape=[jax.ShapeDtypeStruct((n_seq * t_new, Q_W), F32),
                   jax.ShapeDtypeStruct((n_seq, N_KV, t_new, ov.shape[1]), F32)],
        compiler_params=_params(("parallel", "arbitrary"), 48),
        name="nsa_sample_cmp",
    )(page_table, slopes, *([cache] * npg), h, wab, bias, ov)


def _kv_rows(ref, c, g):
    return ref[pl.ds(c * N_KV + g, ref.shape[0] // KV_ROWS, stride=KV_ROWS), :].astype(BF16)


def _pad_keys(new_ref, col, t_new):
    rows = new_ref[:, col:col + HEAD_DIM]
    return jnp.concatenate([rows, jnp.zeros((LANES - t_new, HEAD_DIM), F32)], axis=0).astype(BF16)


def _nsa_sample_sel_kernel(t_new, past, *refs):
    npg = PAGES_PER_STEP
    slopes_ref = refs[1]
    page_refs = refs[2:2 + npg]
    (q_ref, sel_ref, e_ref, oc_ref, ks_new_ref, kw_new_ref, gate_ref, bg_ref, win_ref,
     o_ref, m_ref, l_ref, acc_ref) = refs[2 + npg:]
    j = pl.program_id(1)
    rows = HPG * t_new
    keys = npg * PAGE_SIZE
    qpos_col = _stacked_pos(past, t_new)

    @pl.when(j == 0)
    def _():
        m_ref[...] = jnp.full(m_ref.shape, NEG_INF, F32)
        l_ref[...] = jnp.zeros(l_ref.shape, F32)
        acc_ref[...] = jnp.zeros(acc_ref.shape, F32)

    kpos_row = j * keys + lax.broadcasted_iota(jnp.int32, (1, keys), 1)
    sel_keys = _dot(jnp.concatenate([sel_ref[g] for g in range(N_KV)], axis=0).astype(BF16), e_ref[...])
    for g in range(N_KV):
        q_all = _stack_heads(q_ref.at[:, g * HPG * HEAD_DIM:(g + 1) * HPG * HEAD_DIM], t_new)
        slope_col = _head_cols(slopes_ref, g * HPG, t_new)
        k = jnp.concatenate([_kv_rows(p, 0, g) for p in page_refs], axis=0)
        v = jnp.concatenate([_kv_rows(p, 1, g) for p in page_refs], axis=0)
        extra = _tile_rows(sel_keys[g * t_new:(g + 1) * t_new]) > 0.5
        carry = _flash_tile((m_ref[g], l_ref[g], acc_ref[g]), q_all, k, v, slope_col, qpos_col, kpos_row, extra=extra)
        m_ref[g], l_ref[g], acc_ref[g] = carry

    @pl.when(j == pl.num_programs(1) - 1)
    def _():
        new_pos = past + lax.broadcasted_iota(jnp.int32, (1, LANES), 1)
        wb = win_ref.shape[0] // KV_ROWS
        win_pos = past - wb + lax.broadcasted_iota(jnp.int32, (1, wb), 1)
        for g in range(N_KV):
            q_all = _stack_heads(q_ref.at[:, g * HPG * HEAD_DIM:(g + 1) * HPG * HEAD_DIM], t_new)
            slope_col = _head_cols(slopes_ref, g * HPG, t_new)
            sel_new = _tile_rows(sel_ref[g][:, past // SEL_BLK:past // SEL_BLK + 1]) > 0.5
            carry = _flash_tile((m_ref[g], l_ref[g], acc_ref[g]), q_all,
                                _pad_keys(ks_new_ref, g * HEAD_DIM, t_new),
                                _pad_keys(ks_new_ref, (N_KV + g) * HEAD_DIM, t_new),
                                slope_col, qpos_col, new_pos, extra=sel_new)
            o_s = _flash_out(carry)
            carry = _flash_tile(_flash_init(rows), q_all, _kv_rows(win_ref, 0, g), _kv_rows(win_ref, 1, g),
                                slope_col, qpos_col, win_pos, window=WIN_A)
            carry = _flash_tile(carry, q_all, _pad_keys(kw_new_ref, g * HEAD_DIM, t_new),
                                _pad_keys(kw_new_ref, (N_KV + g) * HEAD_DIM, t_new),
                                slope_col, qpos_col, new_pos, window=WIN_A)
            o_w = _flash_out(carry)
            o_c = jnp.concatenate([oc_ref[:, (g * HPG + hh) * HEAD_DIM:(g * HPG + hh + 1) * HEAD_DIM]
                                   for hh in range(HPG)], axis=0)
            _gated_sum(gate_ref.at[:, g * LANES:(g + 1) * LANES], bg_ref.at[:, g * LANES:(g + 1) * LANES],
                       (o_c, o_s, o_w), o_ref.at[:, g * HPG * HEAD_DIM:(g + 1) * HPG * HEAD_DIM], t_new)


def _nsa_sample_sel(h, row0, cache, lk, page_table, slopes, sel, expand, oc, bg, win, n_seq, t_new, past):
    n_pages = past // PAGE_SIZE
    npg = PAGES_PER_STEP
    nbl = sel.shape[3]
    win_rows = win.shape[2]
    rows = HPG * t_new
    page_spec = lambda k: pl.BlockSpec((None, None, PAGE_SIZE * KV_ROWS, HEAD_DIM),
                                       lambda b, j, pt, k=k: (lk, pt[b, j * npg + k], 0, 0))
    hrow = lambda width, col: pl.BlockSpec((t_new, width), lambda b, j, pt: (row0 // t_new + b, col // width))
    grid_spec = pltpu.PrefetchScalarGridSpec(
        num_scalar_prefetch=1,
        grid=(n_seq, n_pages // npg),
        in_specs=[pl.BlockSpec(memory_space=pltpu.SMEM)] + [page_spec(k) for k in range(npg)] + [
            hrow(Q_W, 0),
            pl.BlockSpec((None, N_KV, t_new, nbl), lambda b, j, pt: (b, 0, 0, 0)),
            pl.BlockSpec((nbl, npg * PAGE_SIZE), lambda b, j, pt: (0, j)),
            pl.BlockSpec((t_new, Q_W), lambda b, j, pt: (b, 0)),
            hrow(KV_W, A_KS), hrow(KV_W, A_KW), hrow(N_KV * LANES, A_GATE),
            pl.BlockSpec((1, N_KV * LANES), lambda b, j, pt: (0, 0)),
            pl.BlockSpec((None, None, win_rows, HEAD_DIM), lambda b, j, pt: (lk, b, 0, 0))],
        out_specs=pl.BlockSpec((t_new, Q_W), lambda b, j, pt: (b, 0)),
        scratch_shapes=[pltpu.VMEM((N_KV, rows, 1), F32), pltpu.VMEM((N_KV, rows, 1), F32),
                        pltpu.VMEM((N_KV, rows, HEAD_DIM), F32)],
    )
    return pl.pallas_call(
        functools.partial(_nsa_sample_sel_kernel, t_new, past),
        grid_spec=grid_spec,
        out_shape=jax.ShapeDtypeStruct((n_seq * t_new, Q_W), F32),
        compiler_params=_params(("parallel", "arbitrary"), 48),
        name="nsa_sample_sel",
    )(page_table, slopes, *([cache] * npg), h, sel, expand, oc, h, h, h, bg, win)


def _swa_sample_kernel(t_new, past, slopes_ref, sinks_ref, q_ref, new_ref, buf_ref, o_ref):
    wb = buf_ref.shape[0] // KV_ROWS
    qpos_col = _stacked_pos(past, t_new)
    buf_pos = past - wb + lax.broadcasted_iota(jnp.int32, (1, wb), 1)
    new_pos = past + lax.broadcasted_iota(jnp.int32, (1, LANES), 1)
    for g in range(N_KV):
        q_all = _stack_heads(q_ref.at[:, g * HPG * HEAD_DIM:(g + 1) * HPG * HEAD_DIM], t_new)
        slope_col = _head_cols(slopes_ref, g * HPG, t_new)
        sink_col = _head_cols(sinks_ref, g * HPG, t_new)
        carry = _flash_tile(_flash_init(HPG * t_new), q_all, _kv_rows(buf_ref, 0, g), _kv_rows(buf_ref, 1, g),
                            slope_col, qpos_col, buf_pos, window=WIN_B)
        carry = _flash_tile(carry, q_all, _pad_keys(new_ref, g * HEAD_DIM, t_new),
                            _pad_keys(new_ref, (N_KV + g) * HEAD_DIM, t_new),
                            slope_col, qpos_col, new_pos, window=WIN_B)
        o = _flash_out_sink(carry, sink_col)
        for hh in range(HPG):
            o_ref[:, (g * HPG + hh) * HEAD_DIM:(g * HPG + hh + 1) * HEAD_DIM] = o[hh * t_new:(hh + 1) * t_new]


def _swa_sample(h, row0, buf, lk, slopes, sinks, n_seq, t_new, past):
    buf_rows = buf.shape[2]
    smem = pl.BlockSpec(memory_space=pltpu.SMEM)
    return pl.pallas_call(
        functools.partial(_swa_sample_kernel, t_new, past),
        grid=(n_seq,),
        in_specs=[smem, smem,
                  pl.BlockSpec((t_new, Q_W), lambda b: (row0 // t_new + b, 0)),
                  pl.BlockSpec((t_new, KV_W), lambda b: (row0 // t_new + b, B_KV // KV_W)),
                  pl.BlockSpec((None, None, buf_rows, HEAD_DIM), lambda b: (lk, b, 0, 0))],
        out_specs=pl.BlockSpec((t_new, Q_W), lambda b: (b, 0)),
        out_shape=jax.ShapeDtypeStruct((n_seq * t_new, Q_W), F32),
        compiler_params=_params(("parallel",), 32),
        name="swa_sample",
    )(slopes, sinks, h, h, buf)


_STAIR = [PEER_TOPK // (k + 1) for k in range(PEER_TOPK)]
_STAIR_ROWS = -(-sum(_STAIR) // 8) * 8


def _top_values(work, out_ref):
    for k in range(PEER_TOPK):
        mx = jnp.max(work, axis=0, keepdims=True)
        out_ref[k:k + 1, :] = mx
        work = jnp.where(work == mx, -jnp.inf, work)


def _peer_route_kernel(x_ref, wqt_ref, sk_ref, xt_ref, b_ref, c_ref, ea_ref, b0_ref,
                       ta_ref, tb_ref, tc_ref, ts_ref, qt_ref):
    xt = x_ref[...].T.astype(BF16)
    xt_ref[...] = xt
    qt_ref[...] = _dot(wqt_ref[...], xt).astype(BF16)
    tc_ref[...] = jnp.full(tc_ref.shape, -jnp.inf, F32)
    n_col = x_ref.shape[0] // LANES

    for hd in range(PEER_HEADS):
        r0 = 2 * hd * PEER_HALF
        b_ref[hd] = _dot(sk_ref[2 * hd + 1], qt_ref[r0 + PEER_HALF:r0 + 2 * PEER_HALF])
        ea_ref[hd] = _dot(sk_ref[2 * hd], qt_ref[r0:r0 + PEER_HALF])

        def column(col, _, hd=hd):
            cs = pl.ds(pl.multiple_of(col * LANES, LANES), LANES)
            s_a = ea_ref[hd, :, cs]
            _top_values(s_a, ta_ref)
            _top_values(b_ref[hd, :, cs], tb_ref)
            top_b = tb_ref[...]
            off = 0
            for k, n_k in enumerate(_STAIR):
                tc_ref[off:off + n_k, :] = ta_ref[k:k + 1, :] + top_b[0:n_k]
                off += n_k
            _top_values(tc_ref[...], ts_ref)
            sc = ts_ref[...]
            thr = sc[PEER_TOPK - 1:PEER_TOPK]
            z = jnp.sum(jnp.exp(sc - sc[0:1]), axis=0, keepdims=True)
            cut_full = jnp.full(s_a.shape, jnp.inf, F32)
            for k, n_k in enumerate(_STAIR):
                a_k = ta_ref[k:k + 1, :]
                cut = jnp.min(jnp.where(a_k + top_b[0:n_k] >= thr, top_b[0:n_k], jnp.inf), axis=0, keepdims=True)
                cut_full = jnp.where(s_a == a_k, cut, cut_full)
            c_ref[hd, :, cs] = cut_full
            ea_ref[hd, :, cs] = jnp.exp(s_a - ta_ref[0:1, :]) / z
            b0_ref[hd:hd + 1, cs] = tb_ref[0:1, :]
            return 0

        lax.fori_loop(0, n_col, column, 0)


def _peer_route(x, wqt, sk, tb):
    n, d = x.shape
    big = jax.ShapeDtypeStruct((PEER_HEADS, PEER_KEYS, n), F32)
    big_spec = pl.BlockSpec((PEER_HEADS, PEER_KEYS, tb), lambda i: (0, 0, i))
    top = pltpu.VMEM((PEER_TOPK, LANES), F32)
    return pl.pallas_call(
        _peer_route_kernel,
        grid=(n // tb,),
        in_specs=[pl.BlockSpec((tb, d), lambda i: (i, 0)), pl.BlockSpec(wqt.shape, lambda i: (0, 0)),
                  pl.BlockSpec(sk.shape, lambda i: (0, 0, 0))],
        out_specs=[pl.BlockSpec((d, tb), lambda i: (0, i))] + [big_spec] * 3 + [
            pl.BlockSpec((PEER_HEADS, tb), lambda i: (0, i))],
        out_shape=[jax.ShapeDtypeStruct((d, n), BF16)] + [big] * 3 + [jax.ShapeDtypeStruct((PEER_HEADS, n), F32)],
        scratch_shapes=[top, top, pltpu.VMEM((_STAIR_ROWS, LANES), F32), top, pltpu.VMEM((wqt.shape[0], tb), BF16)],
        compiler_params=_params(("parallel",), 48),
        name="peer_route",
    )(x, wqt, sk)


def _peer_expert_kernel(alpha, ec, n_chunks, x_ref, xt_ref, b_ref, c_ref, ea_ref, b0_ref, u_ref, vt_ref, g_ref,
                        beta_ref, o_ref, acc_ref, eb_ref, act_a, act_b, hid_a, hid_b):
    j = pl.program_id(1)
    tb = x_ref.shape[0]
    per = ec // PEER_KEYS

    @pl.when(j == 0)
    def _():
        acc_ref[...] = jnp.zeros(acc_ref.shape, F32)
        for ref in (act_a, act_b, hid_a, hid_b):
            ref[...] = jnp.zeros(ref.shape, ref.dtype)
        for hd in range(PEER_HEADS):
            eb_ref[hd] = jnp.exp(b_ref[hd] - b0_ref[hd:hd + 1, :])

    def stages(act_w, act_r, hid_w, hid_r):
        k = j - 1
        valid = jnp.where(jnp.logical_and(k >= 0, k < n_chunks), 1.0, 0.0)
        i0 = jnp.clip(k, 0, n_chunks - 1) * per

        def act_piece(h):
            cols = slice(h * MXU_N, (h + 1) * MXU_N)
            act_w[:, cols] = _dot(u_ref[...], xt_ref[:, cols])

        def acc_piece(r):
            rows = slice(r * ACC_ROWS, (r + 1) * ACC_ROWS)
            acc_ref[rows, :] += _dot(vt_ref[rows, :], hid_r[...])

        c_rows = [[c_ref[hd, pl.ds(i0 + ii, 1), :] for hd in range(PEER_HEADS)] for ii in range(per)]
        ea_rows = [[ea_ref[hd, pl.ds(i0 + ii, 1), :] * valid for hd in range(PEER_HEADS)] for ii in range(per)]

        def gate_tile(col, half):
            cs = slice(col * LANES, (col + 1) * LANES)
            ks = slice(half * GATE_ROWS, (half + 1) * GATE_ROWS)
            w = [None] * per
            for hd in range(PEER_HEADS):
                b = b_ref[hd, ks, cs]
                eb = eb_ref[hd, ks, cs]
                for ii in range(per):
                    t = jnp.where(b >= c_rows[ii][hd][:, cs], eb, 0.0) * ea_rows[ii][hd][:, cs]
                    w[ii] = t if w[ii] is None else w[ii] + t
            for ii in range(per):
                rows = slice(ii * PEER_KEYS + half * GATE_ROWS, ii * PEER_KEYS + (half + 1) * GATE_ROWS)
                act = act_r[rows, cs]
                hid_w[rows, cs] = (0.5 * act * (1.0 + lax.erf(act * SQRT_HALF)) * w[ii]).astype(BF16)

        mxu = [functools.partial(act_piece, h) for h in range(tb // MXU_N)]
        mxu += [functools.partial(acc_piece, r) for r in range(acc_ref.shape[0] // ACC_ROWS)]
        tiles = [(col, half) for col in range(tb // LANES) for half in range(PEER_KEYS // GATE_ROWS)]
        for t, (col, half) in enumerate(tiles):
            for piece in mxu[t * len(mxu) // len(tiles):(t + 1) * len(mxu) // len(tiles)]:
                piece()
            gate_tile(col, half)

    @pl.when(j % 2 == 0)
    def _():
        stages(act_a, act_b, hid_b, hid_a)

    @pl.when(j % 2 == 1)
    def _():
        stages(act_b, act_a, hid_a, hid_b)

    @pl.when(j == pl.num_programs(1) - 1)
    def _():
        o_ref[...] = _layer_norm(alpha * x_ref[...] + acc_ref[...].T, g_ref[...], beta_ref[...])


def _peer_experts(x, route, u, vt, layer, g, beta, alpha, tb, ec):
    n, d = x.shape
    ne = u.shape[1]
    xt, b, c, ea, b0 = route
    n_chunks = ne // ec
    big_spec = pl.BlockSpec((PEER_HEADS, PEER_KEYS, tb), lambda i, j: (0, 0, i))
    act = pltpu.VMEM((ec, tb), F32)
    hid = pltpu.VMEM((ec, tb), BF16)
    return pl.pallas_call(
        functools.partial(_peer_expert_kernel, alpha, ec, n_chunks),
        grid=(n // tb, n_chunks + 2),
        in_specs=[pl.BlockSpec((tb, d), lambda i, j: (i, 0)), pl.BlockSpec((d, tb), lambda i, j: (0, i))]
        + [big_spec] * 3 + [
            pl.BlockSpec((PEER_HEADS, tb), lambda i, j: (0, i)),
            pl.BlockSpec((None, ec, d), lambda i, j: (layer, jnp.minimum(j, n_chunks - 1), 0)),
            pl.BlockSpec((None, d, ec), lambda i, j: (layer, 0, jnp.clip(j - 2, 0, n_chunks - 1))),
            pl.BlockSpec((1, d), lambda i, j: (0, 0)), pl.BlockSpec((1, d), lambda i, j: (0, 0))],
        out_specs=pl.BlockSpec((tb, d), lambda i, j: (i, 0)),
        out_shape=jax.ShapeDtypeStruct((n, d), F32),
        scratch_shapes=[pltpu.VMEM((d, tb), F32), pltpu.VMEM((PEER_HEADS, PEER_KEYS, tb), F32), act, act, hid, hid],
        compiler_params=_params(("parallel", "arbitrary"), 56),
        name="peer_experts",
    )(x, xt, b, c, ea, b0, u, vt, g, beta)


def _overlap_table(n_cmp_pad, n_blk, n_lanes):
    c0 = jnp.arange(n_cmp_pad)[:, None] * CMP_STRIDE
    s0 = jnp.arange(n_lanes)[None, :] * SEL_BLK
    ov = (c0 <= s0 + SEL_BLK - 1) & (c0 + CMP_LEN - 1 >= s0) & (jnp.arange(n_lanes)[None, :] < n_blk)
    return ov.astype(BF16)


def _expand_table(n_lanes, n_pos):
    return (jnp.arange(n_lanes)[:, None] == jnp.arange(n_pos)[None, :] // SEL_BLK).astype(BF16)


def _alibi_slopes():
    i = jnp.arange(1, N_HEADS + 1, dtype=F32)
    return jnp.exp2(-8.0 * i / N_HEADS)


def _round_up(x, m):
    return -(-x // m) * m


def kernel(x_prompt, x_sample, mem_prompt, cache_a_cmp_kv, cache_a_sel_kv, cache_a_win_kv, cache_b_kv, cache_mem_kv, page_table, w_in_a, b_gate_a, w_cmp_a, pe_cmp_a, w_in_b, sinks_b, w_mem_kv, w_out, b_out, ln_g, ln_b, peer_wq, peer_subkeys, peer_u, peer_v):
    batch, seq, d = x_prompt.shape
    n_seq, t_new, _ = x_sample.shape
    depth = w_out.shape[0]
    past = page_table.shape[1] * PAGE_SIZE
    n_prompt = batch * seq
    n_tok = n_prompt + n_seq * t_new
    alpha = (2.0 * depth) ** 0.25
    assert past % SEL_BLK == 0 and t_new <= SEL_BLK and n_tok % TB_PEER == 0 and n_tok % TM_PROJ == 0
    tq = min(TQ_ATTN, seq)
    slopes = _alibi_slopes()

    x = jnp.concatenate([x_prompt.reshape(n_prompt, d), x_sample.reshape(n_seq * t_new, d)], axis=0)
    kv_shape = lambda n, t, g: (n, t, 2, g, HEAD_DIM)
    a_cmp_p, a_cmp_s, a_sel_p, a_sel_s, a_win_p, a_win_s, b_kv_p, b_kv_s, mem_p = ([] for _ in range(9))

    n_pool = cache_a_cmp_kv.shape[1]
    cmp_pages = cache_a_cmp_kv.reshape(-1, n_pool, PAGE_SIZE // CMP_STRIDE, CMP_STRIDE * KV_ROWS, HEAD_DIM)
    sel_pages = cache_a_sel_kv.reshape(-1, n_pool, PAGE_SIZE * KV_ROWS, HEAD_DIM)
    win_rows = cache_a_win_kv.reshape(cache_a_win_kv.shape[0], n_seq, -1, HEAD_DIM)
    buf_rows = cache_b_kv.reshape(cache_b_kv.shape[0], n_seq, -1, HEAD_DIM)
    mem_rows = cache_mem_kv.reshape(depth, n_seq, -1, HEAD_DIM)
    peer_ub = peer_u.astype(BF16)
    peer_vt = jnp.swapaxes(peer_v, 1, 2).astype(BF16)

    def shifted_rows(rows, new):
        new = new.reshape(n_seq, t_new * KV_ROWS, HEAD_DIM)
        out = jnp.concatenate([rows[:, t_new * KV_ROWS:], new], axis=1)
        return out.reshape(kv_shape(n_seq, rows.shape[1] // KV_ROWS, N_KV))

    for li in range(depth):
        lk = li // 2
        n_mem = mem_prompt.shape[1]
        mkv_p = _matmul(mem_prompt.reshape(-1, d), w_mem_kv[li].astype(BF16), batch * n_mem, 2 * QM_W)
        mem_p.append(mkv_p.reshape(batch, n_mem, 2, N_MEM_HEADS, HEAD_DIM))
        mkv_p = mkv_p.reshape(1, batch, n_mem, 2 * QM_W)

        if li % 2 == 0:
            w_in = w_in_a[lk]
            g0 = Q_W + 3 * KV_W
            gate_w = w_in[:, g0:g0 + 3 * N_HEADS].reshape(d, N_KV, 3 * HPG)
            gate_w = jnp.pad(gate_w, ((0, 0), (0, 0), (0, LANES - 3 * HPG))).reshape(d, N_KV * LANES)
            w_r = jnp.concatenate([w_in[:, :g0], w_in[:, g0 + 3 * N_HEADS:], gate_w], axis=1).astype(BF16)
            bg = jnp.pad(b_gate_a[lk].reshape(N_KV, 3 * HPG), ((0, 0), (0, LANES - 3 * HPG))).reshape(1, N_KV * LANES)
            h = _matmul(x, w_r, TM_PROJ, A_WIDTH // 6)

            w_cmp = w_cmp_a[lk].astype(BF16)
            wab = jnp.concatenate([w_cmp[:, :CMP_STRIDE], w_cmp[:, CMP_STRIDE:]], axis=-1)
            wab = wab.reshape(2, CMP_STRIDE // 2, 2 * HEAD_DIM, 2 * HEAD_DIM)
            cmp_bias = _cmp_bias(pe_cmp_a[lk], w_cmp)
            hp = h[:n_prompt].reshape(batch, seq, -1)
            hs = h[n_prompt:].reshape(n_seq, t_new, -1)
            a_cmp_p.append(hp[..., A_KC:A_KC + KV_W].reshape(kv_shape(batch, seq, N_KV)))
            kvc_p = _compress_prompt(a_cmp_p[-1].reshape(-1, HEAD_DIM), wab, cmp_bias, batch, seq)
            n_blk_p = seq // SEL_BLK
            ov_p = _overlap_table(seq // CMP_STRIDE, n_blk_p, _round_up(n_blk_p, LANES))
            ex_p = _expand_table(_round_up(n_blk_p, LANES), seq)
            mix_p = _nsa_prompt(h, kvc_p, slopes, bg, ov_p, ex_p, batch, seq, tq)

            n_blk_s = -(-(past + t_new) // SEL_BLK)
            nbl = _round_up(n_blk_s, LANES)
            ov_s = _overlap_table(past // CMP_STRIDE, n_blk_s, nbl)
            ex_s = _expand_table(nbl, past)
            oc_s, sel_s = _nsa_sample_cmp(h, n_prompt, cmp_pages, lk, page_table, slopes, wab, cmp_bias, ov_s,
                                          n_seq, t_new, past)
            mix_s = _nsa_sample_sel(h, n_prompt, sel_pages, lk, page_table, slopes, sel_s, ex_s, oc_s, bg, win_rows,
                                    n_seq, t_new, past)

            a_cmp_s.append(hs[..., A_KC:A_KC + KV_W].reshape(kv_shape(n_seq, t_new, N_KV)))
            a_sel_p.append(hp[..., A_KS:A_KS + KV_W].reshape(kv_shape(batch, seq, N_KV)))
            a_sel_s.append(hs[..., A_KS:A_KS + KV_W].reshape(kv_shape(n_seq, t_new, N_KV)))
            wn = min(WIN_A, seq)
            a_win_p.append(hp[:, seq - wn:, A_KW:A_KW + KV_W].reshape(kv_shape(batch, wn, N_KV)))
            a_win_s.append(shifted_rows(win_rows[lk], hs[..., A_KW:A_KW + KV_W]))
            qm_col = A_QM
        else:
            h = _matmul(x, w_in_b[lk].astype(BF16), TM_PROJ, B_WIDTH // 4)
            mix_p = _swa_prompt(h, slopes, sinks_b[lk], batch, seq, tq)
            mix_s = _swa_sample(h, n_prompt, buf_rows, lk, slopes, sinks_b[lk], n_seq, t_new, past)
            hp = h[:n_prompt].reshape(batch, seq, -1)
            hs = h[n_prompt:].reshape(n_seq, t_new, -1)
            wn = min(WIN_B, seq)
            b_kv_p.append(hp[:, seq - wn:, B_KV:B_KV + KV_W].reshape(kv_shape(batch, wn, N_KV)))
            b_kv_s.append(shifted_rows(buf_rows[lk], hs[..., B_KV:B_KV + KV_W]))
            qm_col = B_QM

        mem_o_p = _mem_attend(h, qm_col, 0, mkv_p, 0, tq, seq // tq)
        mem_o_s = _mem_attend(h, qm_col, n_prompt, mem_rows, li, t_new, 1, math.gcd(n_seq, MEM_SEQS_PER_STEP))
        mix = jnp.concatenate([mix_p, mix_s], axis=0)
        mem_o = jnp.concatenate([mem_o_p, mem_o_s], axis=0)
        w_o = w_out[li].astype(BF16)
        x = _outproj_ln(mix, mem_o, w_o[:Q_W], w_o[Q_W:], b_out[li].reshape(1, d), x,
                        ln_g[li, 0].reshape(1, d), ln_b[li, 0].reshape(1, d), alpha, TM_OUT)

        sk = peer_subkeys[li].reshape(2 * PEER_HEADS, PEER_KEYS, PEER_HALF).astype(BF16)
        route = _peer_route(x, peer_wq[li].T.astype(BF16), sk, TB_PEER)
        x = _peer_experts(x, route, peer_ub, peer_vt, li, ln_g[li, 1].reshape(1, d), ln_b[li, 1].reshape(1, d),
                          alpha, TB_PEER, EC_PEER)

    return (x[:n_prompt].reshape(batch, seq, d), x[n_prompt:].reshape(n_seq, t_new, d),
            jnp.stack(a_cmp_p), jnp.stack(a_cmp_s), jnp.stack(a_sel_p), jnp.stack(a_sel_s),
            jnp.stack(a_win_p), jnp.stack(a_win_s), jnp.stack(b_kv_p), jnp.stack(b_kv_s), jnp.stack(mem_p))
```
